```python
import math
import jax, jax.numpy as jnp
from jax import lax
import numpy as np


D_MODEL = 2048
BATCH = 8
SEQ = 2048
DEPTH = 1

HEAD_DIM = 128
DIL_CONFIGS = ((128, 1), (512, 4), (2048, 16))
N_DIL_GROUPS = 3
HEADS_PER_DIL_GROUP = 4
HEADS_A = N_DIL_GROUPS * HEADS_PER_DIL_GROUP
N_KEYS_A = 129
HEADS_B = 8
N_ATTN_HEADS = HEADS_A + HEADS_B
IDX_HEADS = 16
IDX_DIM = 64
TOPK_MAX = 256
N_BUCKETS = 32
MAX_DISTANCE = 2048
D_FF = 4 * D_MODEL
WIDTH_A = HEADS_A * HEAD_DIM
WIDTH_B = HEADS_B * HEAD_DIM
WIDTH_A_OUT = HEADS_PER_DIL_GROUP * HEAD_DIM
D_IN = 3 * WIDTH_A + 3 * WIDTH_B + IDX_HEADS * IDX_DIM + IDX_DIM + IDX_HEADS + 2 * D_MODEL
Q_BLOCK = 64
NORM_EPS = 1e-6
NEG_INF = -1e30

kernel_name = "gated_dilated_dsa_hybrid_block"


def rms_norm(x, g):
    xf = x.astype(jnp.float32)
    y = xf * lax.rsqrt(jnp.mean(xf * xf, axis=-1, keepdims=True) + NORM_EPS) * g.astype(jnp.float32)
    return y.astype(x.dtype)


def layer_norm(x, g, b):
    xf = x.astype(jnp.float32)
    mu = jnp.mean(xf, axis=-1, keepdims=True)
    var = jnp.mean(jnp.square(xf - mu), axis=-1, keepdims=True)
    y = (xf - mu) * lax.rsqrt(var + NORM_EPS) * g.astype(jnp.float32) + b.astype(jnp.float32)
    return y.astype(x.dtype)


def t5_bucket(dist):
    n = jnp.maximum(dist, 0)
    max_exact = N_BUCKETS // 2
    nf = jnp.maximum(n, 1).astype(jnp.float32)
    large = max_exact + (jnp.log(nf / max_exact) / math.log(MAX_DISTANCE / max_exact)
                         * (N_BUCKETS - max_exact)).astype(jnp.int32)
    large = jnp.minimum(large, N_BUCKETS - 1)
    return jnp.where(n < max_exact, n, large)


def dilation_offsets():
    return jnp.asarray(np.stack([dil * np.arange(N_KEYS_A) for _, dil in DIL_CONFIGS]), dtype=jnp.int32)


def dilated_mixture_attention(q, k, v, bias_a):
    b, s = q.shape[:2]
    dist = dilation_offsets()
    g_idx = jnp.arange(N_DIL_GROUPS)[:, None, None]
    scale = HEAD_DIM ** -0.5

    def block(i):
        t = i * Q_BLOCK + jnp.arange(Q_BLOCK)
        key_pos = t[None, :, None] - dist[:, None, :]
        valid = key_pos >= 0
        kp = jnp.maximum(key_pos, 0)
        qb = lax.dynamic_slice_in_dim(q, i * Q_BLOCK, Q_BLOCK, axis=1)
        kg = k[:, kp, g_idx]
        vg = v[:, kp, g_idx]
        logits = jnp.einsum('bqghd,bgqjhd->bgqhj', qb, kg,
                            preferred_element_type=jnp.float32) * scale + bias_a[None, :, None]
        logits = jnp.where(valid[None, :, :, None, :], logits, NEG_INF)
        m = jnp.max(logits, axis=-1, keepdims=True)
        e = jnp.exp(logits - m)
        den = jnp.sum(e, axis=-1)
        o = jnp.einsum('bgqhj,bgqjhd->bgqhd', e, vg.astype(jnp.float32)) / den[..., None]
        lse = m[..., 0] + jnp.log(den)
        wts = jax.nn.softmax(lse, axis=1)
        return jnp.einsum('bgqh,bgqhd->bqhd', wts, o)

    out = lax.map(block, jnp.arange(s // Q_BLOCK))
    return out.transpose(1, 0, 2, 3, 4).reshape(b, s, HEADS_PER_DIL_GROUP, HEAD_DIM).astype(q.dtype)


def indexed_sparse_attention(q, k, v, q_idx, k_idx, w_idx, bias_b):
    b, s = q.shape[:2]
    topk = min(TOPK_MAX, s // 4)
    s_pos = jnp.arange(s)
    scale = HEAD_DIM ** -0.5
    idx_scale = IDX_DIM ** -0.5

    def block(i):
        t = i * Q_BLOCK + jnp.arange(Q_BLOCK)
        qi = lax.dynamic_slice_in_dim(q_idx, i * Q_BLOCK, Q_BLOCK, axis=1)
        wi = lax.dynamic_slice_in_dim(w_idx, i * Q_BLOCK, Q_BLOCK, axis=1)
        qb = lax.dynamic_slice_in_dim(q, i * Q_BLOCK, Q_BLOCK, axis=1)
        rel = jax.nn.relu(jnp.einsum('bqhd,bsd->bqsh', qi, k_idx,
                                     preferred_element_type=jnp.float32) * idx_scale)
        score = jnp.einsum('bqsh,bqh->bqs', rel, wi.astype(jnp.float32))
        causal = s_pos[None, :] <= t[:, None]
        score = jnp.where(causal[None], score, NEG_INF)
        _, sel = lax.top_k(score, topk)
        valid = sel <= t[None, :, None]
        kg = jax.vmap(lambda kk, ii: kk[ii])(k, sel)
        vg = jax.vmap(lambda vv, ii: vv[ii])(v, sel)
        logits = jnp.einsum('bqhd,bqkhd->bqkh', qb, kg, preferred_element_type=jnp.float32) * scale
        logits = logits + bias_b[t5_bucket(t[None, :, None] - sel)].astype(jnp.float32)
        logits = jnp.where(valid[..., None], logits, NEG_INF)
        p = jax.nn.softmax(logits, axis=2)
        return jnp.einsum('bqkh,bqkhd->bqhd', p, vg.astype(jnp.float32))

    out = lax.map(block, jnp.arange(s // Q_BLOCK))
    return out.transpose(1, 0, 2, 3, 4).reshape(b, s, HEADS_B, HEAD_DIM).astype(q.dtype)


def split_columns(proj):
    sizes = (3 * WIDTH_A, 3 * WIDTH_B, IDX_HEADS * IDX_DIM, IDX_DIM, IDX_HEADS, D_MODEL, D_MODEL)
    offs = np.cumsum(sizes)[:-1].tolist()
    return jnp.split(proj, offs, axis=-1)


def setup_inputs(seed: int = 0) -> dict:
    key = jax.random.key(seed)
    ks = jax.random.split(key, 14)
    f32 = jnp.float32
    nrm = lambda k, shape: jax.random.normal(k, shape, f32)
    return {
        "x": nrm(ks[0], (BATCH, SEQ, D_MODEL)),
        "norm_mix_g": 1.0 + 0.05 * nrm(ks[1], (DEPTH, D_MODEL)),
        "w_in": nrm(ks[2], (DEPTH, D_MODEL, D_IN)) * D_MODEL ** -0.5,
        "idx_k_norm_g": 1.0 + 0.05 * nrm(ks[3], (DEPTH, IDX_DIM)),
        "idx_k_norm_b": 0.01 * nrm(ks[4], (DEPTH, IDX_DIM)),
        "rel_bias_table": 0.5 * nrm(ks[5], (N_BUCKETS, N_ATTN_HEADS)),
        "w_proj_a": nrm(ks[6], (DEPTH, WIDTH_A_OUT, D_MODEL)) * WIDTH_A_OUT ** -0.5,
        "w_proj_b": nrm(ks[7], (DEPTH, WIDTH_B, D_MODEL)) * WIDTH_B ** -0.5,
        "w_out": nrm(ks[8], (DEPTH, D_MODEL, D_MODEL)) * D_MODEL ** -0.5,
        "norm_mlp_g": 1.0 + 0.05 * nrm(ks[9], (DEPTH, D_MODEL)),
        "w_mlp_up": nrm(ks[10], (DEPTH, D_MODEL, D_FF)) * D_MODEL ** -0.5,
        "w_mlp_down": nrm(ks[11], (DEPTH, D_FF, D_MODEL)) * D_FF ** -0.5,
        "norm_final_g": 1.0 + 0.05 * nrm(ks[12], (D_MODEL,)),
    }


def reference(x, norm_mix_g, w_in, idx_k_norm_g, idx_k_norm_b, rel_bias_table, w_proj_a, w_proj_b,
              w_out, norm_mlp_g, w_mlp_up, w_mlp_down, norm_final_g):
    b, s, _ = x.shape
    head_ids_a = jnp.arange(HEADS_A).reshape(N_DIL_GROUPS, HEADS_PER_DIL_GROUP)
    buckets_a = t5_bucket(dilation_offsets())
    bias_a = rel_bias_table[buckets_a[:, :, None], head_ids_a[:, None, :]]
    bias_a = bias_a.transpose(0, 2, 1).astype(jnp.float32)
    bias_b = rel_bias_table[:, HEADS_A:]

    for layer in range(DEPTH):
        h = rms_norm(x, norm_mix_g[layer])
        qkv_a, qkv_b, q_idx, k_idx, w_idx, gate_a, gate_b = split_columns(h @ w_in[layer])
        qkv_a = qkv_a.reshape(b, s, 3, N_DIL_GROUPS, HEADS_PER_DIL_GROUP, HEAD_DIM)
        qkv_b = qkv_b.reshape(b, s, 3, HEADS_B, HEAD_DIM)
        o_a = dilated_mixture_attention(qkv_a[:, :, 0], qkv_a[:, :, 1], qkv_a[:, :, 2], bias_a)
        q_idx = q_idx.reshape(b, s, IDX_HEADS, IDX_DIM)
        k_idx = layer_norm(k_idx, idx_k_norm_g[layer], idx_k_norm_b[layer])
        w_idx = w_idx * IDX_HEADS ** -0.5
        o_b = indexed_sparse_attention(qkv_b[:, :, 0], qkv_b[:, :, 1], qkv_b[:, :, 2],
                                       q_idx, k_idx, w_idx, bias_b)
        merged = (jax.nn.sigmoid(gate_a) * (o_a.reshape(b, s, WIDTH_A_OUT) @ w_proj_a[layer])
                  + jax.nn.sigmoid(gate_b) * (o_b.reshape(b, s, WIDTH_B) @ w_proj_b[layer]))
        x = x + merged @ w_out[layer]
        hm = rms_norm(x, norm_mlp_g[layer])
        x = x + jnp.square(jax.nn.relu(hm @ w_mlp_up[layer])) @ w_mlp_down[layer]
    return rms_norm(x, norm_final_g)
```

```python
import functools
import math

import jax
import jax.numpy as jnp
import numpy as np
from jax import lax
from jax.experimental import pallas as pl
from jax.experimental.pallas import tpu as pltpu

HEAD_DIM = 128
DILATIONS = (1, 4, 16)
N_GROUPS = 3
HEADS_PER_GROUP = 4
HEADS_A = N_GROUPS * HEADS_PER_GROUP
BAND = 128
HEADS_B = 8
N_HEADS = HEADS_A + HEADS_B
IDX_HEADS = 16
IDX_DIM = 64
TOPK = 256
N_BUCKETS = 32
MAX_DISTANCE = 2048
NORM_EPS = 1e-6
NEG_INF = -1e30
INT_MIN = -(2**31)

WIDTH_A = HEADS_A * HEAD_DIM
WIDTH_B = HEADS_B * HEAD_DIM
GROUP_W = HEADS_PER_GROUP * HEAD_DIM
IDX_W = IDX_HEADS * IDX_DIM
BIG_W = 3 * WIDTH_B + IDX_W + 3 * WIDTH_A
QKV_A_UNIT0 = (3 * WIDTH_B + IDX_W) // GROUP_W
BIG_UNITS = BIG_W // GROUP_W
SMALL_W = 128

V7X_VMEM_LIMIT = 56 * 1024 * 1024

_F32 = jnp.float32
_BF16 = jnp.bfloat16


def _bucket_bounds():
    n = np.arange(MAX_DISTANCE, dtype=np.int32)
    max_exact = N_BUCKETS // 2
    nf = np.maximum(n, 1).astype(np.float32)
    large = max_exact + (
        np.log(nf / np.float32(max_exact)) / np.float32(math.log(MAX_DISTANCE / max_exact))
        * np.float32(N_BUCKETS - max_exact)
    ).astype(np.int32)
    large = np.minimum(large, N_BUCKETS - 1)
    bucket = np.where(n < max_exact, n, large)
    assert np.all(np.diff(bucket) >= 0)
    bounds = [int(np.argmax(bucket >= b)) if np.any(bucket >= b) else MAX_DISTANCE for b in range(N_BUCKETS)]
    return tuple(bounds)


_BOUNDS = _bucket_bounds()


def _dot_nt(a, b):
    return lax.dot_general(a, b, (((1,), (1,)), ((), ())), preferred_element_type=_F32)


def _bias_from_distance(tab_ref, dist, head):
    val = jnp.full(dist.shape, tab_ref[head], _F32)
    for bkt in range(1, N_BUCKETS):
        val = jnp.where(dist >= _BOUNDS[bkt], tab_ref[bkt * N_HEADS + head], val)
    return val


def _norm_matmul_kernel(x_ref, g_ref, w_ref, o_ref, h_ref, *, tm):
    @pl.when(pl.program_id(1) == 0)
    def _():
        def chunk(c, carry):
            r = pl.multiple_of(c * 128, 128)
            x = x_ref[pl.ds(r, 128), :]
            ms = jnp.mean(x * x, axis=-1, keepdims=True)
            h_ref[pl.ds(r, 128), :] = (x * lax.rsqrt(ms + NORM_EPS) * g_ref[...]).astype(_BF16)
            return carry

        lax.fori_loop(0, tm // 128, chunk, 0)

    o_ref[...] = jnp.dot(h_ref[...], w_ref[...], preferred_element_type=_F32).astype(o_ref.dtype)


def _norm_matmul(x2d, g, w, out_dtype, tm, tn):
    n, d = x2d.shape
    cols = w.shape[1]
    return pl.pallas_call(
        functools.partial(_norm_matmul_kernel, tm=tm),
        grid=(n // tm, cols // tn),
        in_specs=[
            pl.BlockSpec((tm, d), lambda i, j: (i, 0)),
            pl.BlockSpec((1, d), lambda i, j: (0, 0)),
            pl.BlockSpec((d, tn), lambda i, j: (0, j)),
        ],
        out_specs=pl.BlockSpec((tm, tn), lambda i, j: (i, j)),
        out_shape=jax.ShapeDtypeStruct((n, cols), out_dtype),
        scratch_shapes=[pltpu.VMEM((tm, d), _BF16)],
        compiler_params=pltpu.CompilerParams(
            dimension_semantics=("arbitrary", "arbitrary"), vmem_limit_bytes=V7X_VMEM_LIMIT),
        name="norm_in_proj",
    )(x2d, g, w)


def _mixer_a_kernel(tab_ref, q_ref, k_ref, v_ref, o_ref, lse_ref, bias_ref, *, group, seq):
    dil = DILATIONS[group]
    nq = seq // BAND
    scale = HEAD_DIM ** -0.5

    @pl.when((pl.program_id(0) == 0) & (pl.program_id(1) == 0))
    def _build_bias():
        row = lax.broadcasted_iota(jnp.int32, (BAND, BAND), 0)
        col = lax.broadcasted_iota(jnp.int32, (BAND, BAND), 1)
        for side in range(2):
            j = row - col + BAND * side
            valid = jnp.where(j >= 0, j, BAND + 1) <= BAND
            for h in range(HEADS_PER_GROUP):
                val = _bias_from_distance(tab_ref, j * dil, group * HEADS_PER_GROUP + h)
                bias_ref[h, side] = jnp.where(valid, val, NEG_INF)

    def qtile(i, carry):
        r0 = pl.multiple_of(i * BAND, BAND)
        rp = pl.multiple_of(jnp.maximum(i - 1, 0) * BAND, BAND)
        for h in range(HEADS_PER_GROUP):
            cs = slice(h * HEAD_DIM, (h + 1) * HEAD_DIM)
            q = q_ref[0, pl.ds(r0, BAND), cs]
            s_c = _dot_nt(q, k_ref[0, pl.ds(r0, BAND), cs]) * scale + bias_ref[h, 0]
            m = jnp.max(s_c, axis=-1, keepdims=True)
            if nq > 1:
                s_p = _dot_nt(q, k_ref[0, pl.ds(rp, BAND), cs]) * scale + bias_ref[h, 1]
                s_p = jnp.where(i > 0, s_p, NEG_INF)
                m = jnp.maximum(m, jnp.max(s_p, axis=-1, keepdims=True))
            p_c = jnp.exp(s_c - m)
            den = jnp.sum(p_c, axis=-1, keepdims=True)
            num = jnp.dot(p_c.astype(_BF16), v_ref[0, pl.ds(r0, BAND), cs], preferred_element_type=_F32)
            if nq > 1:
                p_p = jnp.exp(s_p - m)
                den = den + jnp.sum(p_p, axis=-1, keepdims=True)
                num = num + jnp.dot(p_p.astype(_BF16), v_ref[0, pl.ds(rp, BAND), cs],
                                    preferred_element_type=_F32)
            o_ref[0, pl.ds(r0, BAND), cs] = num / den
            lse_ref[0, pl.ds(r0, BAND), cs] = jnp.broadcast_to(m + jnp.log(den), (BAND, HEAD_DIM))
        return carry

    lax.fori_loop(0, nq, qtile, 0)


def _mixer_a_group(tab, big, group, batch, seq_total):
    dil = DILATIONS[group]
    seq = seq_total // dil
    big_v = big.reshape(batch, seq, dil * BIG_W)
    unit = QKV_A_UNIT0 + group

    def in_spec(which):
        return pl.BlockSpec((1, seq, GROUP_W),
                            lambda b, r, which=which: (b, 0, r * BIG_UNITS + unit + N_GROUPS * which))

    out_spec = pl.BlockSpec((1, seq, GROUP_W), lambda b, r: (b, 0, r))
    out_sds = jax.ShapeDtypeStruct((batch, seq, dil * GROUP_W), _F32)
    o, lse = pl.pallas_call(
        functools.partial(_mixer_a_kernel, group=group, seq=seq),
        grid=(batch, dil),
        in_specs=[pl.BlockSpec(memory_space=pltpu.SMEM), in_spec(0), in_spec(1), in_spec(2)],
        out_specs=[out_spec, out_spec],
        out_shape=[out_sds, out_sds],
        scratch_shapes=[pltpu.VMEM((HEADS_PER_GROUP, 2, BAND, BAND), _F32)],
        compiler_params=pltpu.CompilerParams(
            dimension_semantics=("arbitrary", "arbitrary"), vmem_limit_bytes=V7X_VMEM_LIMIT),
        name=f"mixer_a_g{group}",
    )(tab, big_v, big_v, big_v)
    n = batch * seq_total
    return o.reshape(n, GROUP_W), lse.reshape(n, GROUP_W)


def _mixer_b_kernel(tab_ref, lng_ref, lnb_ref, q_ref, k_ref, v_ref, qidx_ref, smallk_ref, smallq_ref,
                    o_ref, kidx_s, bias_s, key_s, selb_s, wrep_s, *, tq, seq):
    b = pl.program_id(0)
    qi = pl.program_id(1)
    nkb = seq // tq
    lane_tiles = tq // 128
    scale = HEAD_DIM ** -0.5
    row = lax.broadcasted_iota(jnp.int32, (tq, tq), 0)
    col = lax.broadcasted_iota(jnp.int32, (tq, tq), 1)

    @pl.when((b == 0) & (qi == 0))
    def _build_bias():
        def per_delta(dl, carry):
            dist = row - col + dl * tq
            for h in range(HEADS_B):
                bias_s[dl, h] = _bias_from_distance(tab_ref, dist, HEADS_A + h)
            return carry

        lax.fori_loop(0, nkb, per_delta, 0)

    @pl.when(qi == 0)
    def _layer_norm_keys():
        def chunk(c, carry):
            r = pl.multiple_of(c * tq, tq)
            kx = smallk_ref[0, pl.ds(r, tq), :][:, :IDX_DIM]
            mu = jnp.mean(kx, axis=-1, keepdims=True)
            xc = kx - mu
            var = jnp.mean(xc * xc, axis=-1, keepdims=True)
            y = (xc * lax.rsqrt(var + NORM_EPS) * lng_ref[...] + lnb_ref[...]).astype(_BF16)
            z = jnp.zeros_like(y)
            kidx_s[0, pl.ds(r, tq), :] = jnp.concatenate([y, z], axis=-1)
            kidx_s[1, pl.ds(r, tq), :] = jnp.concatenate([z, y], axis=-1)
            return carry

        lax.fori_loop(0, nkb, chunk, 0)

    wq = smallq_ref[0][:, IDX_DIM:IDX_DIM + IDX_HEADS] * (IDX_HEADS ** -0.5 * IDX_DIM ** -0.5)
    for h in range(IDX_HEADS):
        wrep_s[h] = jnp.broadcast_to(wq[:, h:h + 1], (tq, 128))

    t_row = qi * tq + row

    def score_blk(kj, carry):
        r = pl.multiple_of(kj * tq, tq)
        acc = jnp.zeros((tq, tq), _F32)
        for h in range(IDX_HEADS):
            pair = qidx_ref[0][:, (h // 2) * 128:(h // 2 + 1) * 128]
            x = _dot_nt(pair, kidx_s[h % 2, pl.ds(r, tq), :])
            w = wrep_s[h]
            acc = acc + jnp.concatenate([w] * lane_tiles, axis=-1) * jnp.maximum(x, 0.0)
        bits = lax.bitcast_convert_type(acc, jnp.int32)
        key = bits ^ ((bits >> 31) & 0x7FFFFFFF)
        causal = (kj * tq + col) <= t_row
        key_s[kj] = jnp.where(causal, key, INT_MIN)
        return carry

    lax.fori_loop(0, qi + 1, score_blk, 0)

    rc_rows = 64

    def row_chunk(rc, carry):
        r = pl.multiple_of(rc * rc_rows, rc_rows)

        def count_ge(cand):
            def blk(kj, cnt):
                kk = key_s[kj, pl.ds(r, rc_rows), :]
                for c in range(lane_tiles):
                    cnt = cnt + jnp.where(kk[:, c * 128:(c + 1) * 128] >= cand, 1.0, 0.0)
                return cnt

            cnt = lax.fori_loop(0, qi + 1, blk, jnp.zeros((rc_rows, 128), _F32))
            return jnp.broadcast_to(jnp.sum(cnt, axis=-1, keepdims=True), (rc_rows, 128))

        zero = jnp.zeros((rc_rows, 128), jnp.int32)
        prefix = jnp.where(count_ge(zero) >= TOPK, zero, INT_MIN)

        def bit_body(bi, prefix):
            cand = prefix | lax.shift_left(jnp.int32(1), 30 - bi)
            return jnp.where(count_ge(cand) >= TOPK, cand, prefix)

        tau = lax.fori_loop(0, 31, bit_body, prefix)
        thr = jnp.maximum(tau, INT_MIN + 1)

        def write(kj, c2):
            kk = key_s[kj, pl.ds(r, rc_rows), :]
            selb_s[kj, pl.ds(r, rc_rows), :] = jnp.where(
                kk >= jnp.concatenate([thr] * lane_tiles, axis=-1), 0.0, NEG_INF)
            return c2

        lax.fori_loop(0, qi + 1, write, 0)
        return carry

    lax.fori_loop(0, tq // rc_rows, row_chunk, 0)

    for h in range(HEADS_B):
        cs = slice(h * HEAD_DIM, (h + 1) * HEAD_DIM)
        q = q_ref[0][:, cs]

        def blk(kj, carry, h=h, cs=cs, q=q):
            m, l, acc = carry
            r = pl.multiple_of(kj * tq, tq)
            s = _dot_nt(q, k_ref[0, pl.ds(r, tq), cs]) * scale + bias_s[qi - kj, h] + selb_s[kj]
            m_new = jnp.maximum(m, jnp.max(s, axis=-1, keepdims=True))
            alpha = jnp.exp(m - m_new)
            p = jnp.exp(s - m_new)
            l = alpha * l + jnp.sum(p, axis=-1, keepdims=True)
            acc = alpha * acc + jnp.dot(p.astype(_BF16), v_ref[0, pl.ds(r, tq), cs],
                                        preferred_element_type=_F32)
            return m_new, l, acc

        init = (jnp.full((tq, 1), -3e38, _F32), jnp.zeros((tq, 1), _F32), jnp.zeros((tq, HEAD_DIM), _F32))
        m, l, acc = lax.fori_loop(0, qi + 1, blk, init)
        o_ref[0, :, cs] = (acc / l).astype(o_ref.dtype)


def _mixer_b(tab, ln_g, ln_b, big3, small3, tq):
    batch, seq, _ = big3.shape
    nkb = seq // tq
    full = lambda cb: pl.BlockSpec((1, seq, WIDTH_B), lambda b, i, cb=cb: (b, 0, cb))
    tile = lambda cb: pl.BlockSpec((1, tq, WIDTH_B), lambda b, i, cb=cb: (b, i, cb))
    small_cb = small3.shape[-1] // SMALL_W - 1
    return pl.pallas_call(
        functools.partial(_mixer_b_kernel, tq=tq, seq=seq),
        grid=(batch, nkb),
        in_specs=[
            pl.BlockSpec(memory_space=pltpu.SMEM),
            pl.BlockSpec((1, IDX_DIM), lambda b, i: (0, 0)),
            pl.BlockSpec((1, IDX_DIM), lambda b, i: (0, 0)),
            tile(0), full(1), full(2), tile(3),
            pl.BlockSpec((1, seq, SMALL_W), lambda b, i: (b, 0, small_cb)),
            pl.BlockSpec((1, tq, SMALL_W), lambda b, i: (b, i, small_cb)),
        ],
        out_specs=pl.BlockSpec((1, tq, WIDTH_B), lambda b, i: (b, i, 0)),
        out_shape=jax.ShapeDtypeStruct((batch, seq, WIDTH_B), _BF16),
        scratch_shapes=[
            pltpu.VMEM((2, seq, 2 * IDX_DIM), _BF16),
            pltpu.VMEM((nkb, HEADS_B, tq, tq), _F32),
            pltpu.VMEM((nkb, tq, tq), jnp.int32),
            pltpu.VMEM((nkb, tq, tq), _F32),
            pltpu.VMEM((IDX_HEADS, tq, 128), _F32),
        ],
        compiler_params=pltpu.CompilerParams(
            dimension_semantics=("arbitrary", "arbitrary"), vmem_limit_bytes=V7X_VMEM_LIMIT),
        name="mixer_b",
    )(tab, ln_g, ln_b, big3, big3, big3, big3, small3, small3)


def _merge_kernel(x_ref, oa0_ref, oa1_ref, oa2_ref, l0_ref, l1_ref, l2_ref, ob_ref, ga_ref, gb_ref,
                  wpa_ref, wpb_ref, wout_ref, y_ref):
    l0, l1, l2 = l0_ref[...], l1_ref[...], l2_ref[...]
    m = jnp.maximum(jnp.maximum(l0, l1), l2)
    e0, e1, e2 = jnp.exp(l0 - m), jnp.exp(l1 - m), jnp.exp(l2 - m)
    o_a = (e0 * oa0_ref[...] + e1 * oa1_ref[...] + e2 * oa2_ref[...]) / (e0 + e1 + e2)
    pa = jnp.dot(o_a.astype(_BF16), wpa_ref[...], preferred_element_type=_F32)
    pb = jnp.dot(ob_ref[...], wpb_ref[...], preferred_element_type=_F32)
    merged = jax.nn.sigmoid(ga_ref[...]) * pa + jax.nn.sigmoid(gb_ref[...]) * pb
    y_ref[...] = x_ref[...] + jnp.dot(merged.astype(_BF16), wout_ref[...], preferred_element_type=_F32)


def _merge(x2d, oa, lse, ob2d, fout, wpa, wpb, wout, tm):
    n, d = x2d.shape
    rows = lambda w, cb=0: pl.BlockSpec((tm, w), lambda i, cb=cb: (i, cb))
    const = lambda shape: pl.BlockSpec(shape, lambda i: (0, 0), pipeline_mode=pl.Buffered(1))
    return pl.pallas_call(
        _merge_kernel,
        grid=(n // tm,),
        in_specs=[rows(d), rows(GROUP_W), rows(GROUP_W), rows(GROUP_W), rows(GROUP_W), rows(GROUP_W),
                  rows(GROUP_W), rows(WIDTH_B), rows(d, 0), rows(d, 1),
                  const(wpa.shape), const(wpb.shape), const(wout.shape)],
        out_specs=rows(d),
        out_shape=jax.ShapeDtypeStruct((n, d), _F32),
        compiler_params=pltpu.CompilerParams(
            dimension_semantics=("arbitrary",), vmem_limit_bytes=V7X_VMEM_LIMIT),
        name="merge_out_proj",
    )(x2d, oa[0], oa[1], oa[2], lse[0], lse[1], lse[2], ob2d, fout, fout, wpa, wpb, wout)


def _mlp_kernel(x_ref, g_ref, gf_ref, wu_ref, wd_ref, y_ref, h_ref, acc_ref):
    j = pl.program_id(1)

    @pl.when(j == 0)
    def _():
        x = x_ref[...]
        ms = jnp.mean(x * x, axis=-1, keepdims=True)
        h_ref[...] = (x * lax.rsqrt(ms + NORM_EPS) * g_ref[...]).astype(_BF16)
        acc_ref[...] = jnp.zeros_like(acc_ref)

    u = jnp.maximum(jnp.dot(h_ref[...], wu_ref[...], preferred_element_type=_F32), 0.0)
    acc_ref[...] += jnp.dot((u * u).astype(_BF16), wd_ref[...], preferred_element_type=_F32)

    @pl.when(j == pl.num_programs(1) - 1)
    def _():
        y = x_ref[...] + acc_ref[...]
        ms = jnp.mean(y * y, axis=-1, keepdims=True)
        y_ref[...] = y * lax.rsqrt(ms + NORM_EPS) * gf_ref[...]


def _mlp(x2d, g, gf, wu, wd, tm, tf):
    n, d = x2d.shape
    dff = wu.shape[1]
    return pl.pallas_call(
        _mlp_kernel,
        grid=(n // tm, dff // tf),
        in_specs=[
            pl.BlockSpec((tm, d), lambda i, j: (i, 0)),
            pl.BlockSpec((1, d), lambda i, j: (0, 0)),
            pl.BlockSpec((1, d), lambda i, j: (0, 0)),
            pl.BlockSpec((d, tf), lambda i, j: (0, j)),
            pl.BlockSpec((tf, d), lambda i, j: (j, 0)),
        ],
        out_specs=pl.BlockSpec((tm, d), lambda i, j: (i, 0)),
        out_shape=jax.ShapeDtypeStruct((n, d), _F32),
        scratch_shapes=[pltpu.VMEM((tm, d), _BF16), pltpu.VMEM((tm, d), _F32)],
        compiler_params=pltpu.CompilerParams(
            dimension_semantics=("arbitrary", "arbitrary"), vmem_limit_bytes=V7X_VMEM_LIMIT),
        name="mlp_final_norm",
    )(x2d, g, gf, wu, wd)


def _largest_tile(n, limit, step):
    t = (min(n, limit) // step) * step
    while n % t:
        t -= step
    return t


def kernel(x, norm_mix_g, w_in, idx_k_norm_g, idx_k_norm_b, rel_bias_table, w_proj_a, w_proj_b, w_out,
           norm_mlp_g, w_mlp_up, w_mlp_down, norm_final_g):
    batch, seq, d = x.shape
    assert seq == MAX_DISTANCE and seq % (DILATIONS[-1] * BAND) == 0
    assert w_in.shape[0] == 1, "single layer"
    n = batch * seq
    x2d = x.reshape(n, d)
    tab = rel_bias_table.reshape(-1)

    w = w_in[0]
    a0, b0 = 0, 3 * WIDTH_A
    q0 = b0 + 3 * WIDTH_B
    k0 = q0 + IDX_W
    wi0 = k0 + IDX_DIM
    ga0 = wi0 + IDX_HEADS
    w_big = jnp.concatenate([w[:, b0:q0], w[:, q0:k0], w[:, a0:b0]], axis=1).astype(_BF16)
    w_f32out = jnp.concatenate(
        [w[:, ga0:ga0 + 2 * d], w[:, k0:ga0], jnp.zeros((d, SMALL_W - IDX_DIM - IDX_HEADS), w.dtype)],
        axis=1).astype(_BF16)
    g_mix = norm_mix_g[0].reshape(1, d)

    tm = _largest_tile(n, 1024, 128)
    big = _norm_matmul(x2d, g_mix, w_big, _BF16, tm, _largest_tile(BIG_W, 512, 128))
    fout = _norm_matmul(x2d, g_mix, w_f32out, _F32, tm, _largest_tile(w_f32out.shape[1], 1408, 128))

    oa, lse = [], []
    for group in range(N_GROUPS):
        o_g, lse_g = _mixer_a_group(tab, big, group, batch, seq)
        oa.append(o_g)
        lse.append(lse_g)

    ob = _mixer_b(tab, idx_k_norm_g[0].reshape(1, IDX_DIM), idx_k_norm_b[0].reshape(1, IDX_DIM),
                  big.reshape(batch, seq, BIG_W), fout.reshape(batch, seq, -1), tq=256)

    x1 = _merge(x2d, oa, lse, ob.reshape(n, WIDTH_B), fout,
                w_proj_a[0].astype(_BF16), w_proj_b[0].astype(_BF16), w_out[0].astype(_BF16),
                tm=_largest_tile(n, 256, 128))
    y = _mlp(x1, norm_mlp_g[0].reshape(1, d), norm_final_g.reshape(1, d),
             w_mlp_up[0].astype(_BF16), w_mlp_down[0].astype(_BF16),
             tm=_largest_tile(n, 512, 128), tf=_largest_tile(w_mlp_up.shape[-1], 512, 128))
    return y.reshape(batch, seq, d)
```

```python
import functools
import math

import jax
import jax.numpy as jnp
import numpy as np
from jax import lax
from jax.experimental import pallas as pl
from jax.experimental.pallas import tpu as pltpu

HEAD_DIM = 128
DILATIONS = (1, 4, 16)
N_GROUPS = 3
HEADS_PER_GROUP = 4
HEADS_A = N_GROUPS * HEADS_PER_GROUP
BAND = 128
HEADS_B = 8
N_HEADS = HEADS_A + HEADS_B
IDX_HEADS = 16
IDX_DIM = 64
TOPK = 256
N_BUCKETS = 32
MAX_DISTANCE = 2048
NORM_EPS = 1e-6
NEG_INF = -1e30
INT_MIN = -(2**31)

WIDTH_A = HEADS_A * HEAD_DIM
WIDTH_B = HEADS_B * HEAD_DIM
GROUP_W = HEADS_PER_GROUP * HEAD_DIM
IDX_W = IDX_HEADS * IDX_DIM
SMALL_W = 128

V7X_VMEM_LIMIT = 56 * 1024 * 1024

_F32 = jnp.float32
_BF16 = jnp.bfloat16


def _bucket_bounds():
    n = np.arange(MAX_DISTANCE, dtype=np.int32)
    max_exact = N_BUCKETS // 2
    nf = np.maximum(n, 1).astype(np.float32)
    large = max_exact + (
        np.log(nf / np.float32(max_exact)) / np.float32(math.log(MAX_DISTANCE / max_exact))
        * np.float32(N_BUCKETS - max_exact)
    ).astype(np.int32)
    large = np.minimum(large, N_BUCKETS - 1)
    bucket = np.where(n < max_exact, n, large)
    assert np.all(np.diff(bucket) >= 0)
    bounds = [int(np.argmax(bucket >= b)) if np.any(bucket >= b) else MAX_DISTANCE for b in range(N_BUCKETS)]
    return tuple(bounds)


_BOUNDS = _bucket_bounds()


def _dot_nt(a, b):
    return lax.dot_general(a, b, (((1,), (1,)), ((), ())), preferred_element_type=_F32)


def _bias_from_distance(tab_ref, dist, head):
    val = jnp.full(dist.shape, tab_ref[head], _F32)
    for bkt in range(1, N_BUCKETS):
        val = jnp.where(dist >= _BOUNDS[bkt], tab_ref[bkt * N_HEADS + head], val)
    return val


def _rmsnorm_kernel(x_ref, g_ref, h_ref):
    x = x_ref[...]
    ms = jnp.mean(x * x, axis=-1, keepdims=True)
    h_ref[...] = (x * lax.rsqrt(ms + NORM_EPS) * g_ref[...]).astype(h_ref.dtype)


def _rmsnorm(x2d, g, tm):
    n, d = x2d.shape
    return pl.pallas_call(
        _rmsnorm_kernel,
        grid=(n // tm,),
        in_specs=[pl.BlockSpec((tm, d), lambda i: (i, 0)), pl.BlockSpec((1, d), lambda i: (0, 0))],
        out_specs=pl.BlockSpec((tm, d), lambda i: (i, 0)),
        out_shape=jax.ShapeDtypeStruct((n, d), _BF16),
        compiler_params=pltpu.CompilerParams(
            dimension_semantics=("arbitrary",), vmem_limit_bytes=V7X_VMEM_LIMIT),
        name="rmsnorm_mix",
    )(x2d, g)


def _matmul_kernel(h_ref, w_ref, o_ref):
    o_ref[...] = jnp.dot(h_ref[...], w_ref[...], preferred_element_type=_F32).astype(o_ref.dtype)


def _matmul(h, w, out_dtype, tm, tn, name):
    n, d = h.shape
    cols = w.shape[1]
    return pl.pallas_call(
        _matmul_kernel,
        grid=(n // tm, cols // tn),
        in_specs=[pl.BlockSpec((tm, d), lambda i, j: (i, 0)), pl.BlockSpec((d, tn), lambda i, j: (0, j))],
        out_specs=pl.BlockSpec((tm, tn), lambda i, j: (i, j)),
        out_shape=jax.ShapeDtypeStruct((n, cols), out_dtype),
        compiler_params=pltpu.CompilerParams(
            dimension_semantics=("arbitrary", "arbitrary"), vmem_limit_bytes=V7X_VMEM_LIMIT),
        name=name,
    )(h, w)


def _matmul_classmajor_kernel(h_ref, w_ref, o_ref, acc_ref, *, dil, rows):
    res = jnp.dot(h_ref[...], w_ref[...], preferred_element_type=_F32)
    if dil == 1:
        o_ref[0, 0] = res.astype(o_ref.dtype)
        return
    for c in range(acc_ref.shape[0]):
        acc_ref[c] = res[:, c * 128:(c + 1) * 128]
    for c in range(acc_ref.shape[0]):
        for r in range(dil):
            o_ref[0, r, :, c * 128:(c + 1) * 128] = (
                acc_ref[c, pl.ds(r, rows, stride=dil), :].astype(o_ref.dtype))


def _matmul_classmajor(h, w, dil, batch, seq, tm, tn, name):
    n, d = h.shape
    cols = w.shape[1]
    tiles_per_seq = seq // tm
    rows = tm // dil
    return pl.pallas_call(
        functools.partial(_matmul_classmajor_kernel, dil=dil, rows=rows),
        grid=(n // tm, cols // tn),
        in_specs=[pl.BlockSpec((tm, d), lambda i, j: (i, 0)), pl.BlockSpec((d, tn), lambda i, j: (0, j))],
        out_specs=pl.BlockSpec((1, dil, rows, tn),
                               lambda i, j: (i // tiles_per_seq, 0, i % tiles_per_seq, j)),
        out_shape=jax.ShapeDtypeStruct((batch, dil, seq // dil, cols), _BF16),
        scratch_shapes=[pltpu.VMEM((tn // 128, tm, 128), _F32)],
        compiler_params=pltpu.CompilerParams(
            dimension_semantics=("arbitrary", "arbitrary"), vmem_limit_bytes=V7X_VMEM_LIMIT),
        name=name,
    )(h, w)


def _mixer_a_kernel(tab_ref, a_ref, o_ref, lse_ref, bias_ref, *, group, seq):
    dil = DILATIONS[group]
    nq = seq // dil // BAND
    scale = HEAD_DIM ** -0.5

    @pl.when(pl.program_id(0) == 0)
    def _build_bias():
        row = lax.broadcasted_iota(jnp.int32, (BAND, BAND), 0)
        col = lax.broadcasted_iota(jnp.int32, (BAND, BAND), 1)
        for side in range(2):
            j = row - col + BAND * side
            valid = jnp.where(j >= 0, j, BAND + 1) <= BAND
            for h in range(HEADS_PER_GROUP):
                val = _bias_from_distance(tab_ref, j * dil, group * HEADS_PER_GROUP + h)
                bias_ref[h, side] = jnp.where(valid, val, NEG_INF)

    def tile(r, i):
        static = isinstance(i, int)
        r0 = i * BAND if static else pl.multiple_of(i * BAND, BAND)
        has_prev = nq > 1 and not (static and i == 0)
        if has_prev:
            rp = (i - 1) * BAND if static else pl.multiple_of(jnp.maximum(i - 1, 0) * BAND, BAND)
        out_rows = pl.ds(r0, BAND) if dil == 1 else pl.ds(r + r0 * dil, BAND, stride=dil)
        for h in range(HEADS_PER_GROUP):
            qc = slice(h * HEAD_DIM, (h + 1) * HEAD_DIM)
            kc = slice(GROUP_W + h * HEAD_DIM, GROUP_W + (h + 1) * HEAD_DIM)
            vc = slice(2 * GROUP_W + h * HEAD_DIM, 2 * GROUP_W + (h + 1) * HEAD_DIM)
            q = a_ref[0, r, pl.ds(r0, BAND), qc]
            s_c = _dot_nt(q, a_ref[0, r, pl.ds(r0, BAND), kc]) * scale + bias_ref[h, 0]
            m = jnp.max(s_c, axis=-1, keepdims=True)
            if has_prev:
                s_p = _dot_nt(q, a_ref[0, r, pl.ds(rp, BAND), kc]) * scale + bias_ref[h, 1]
                if not static:
                    s_p = jnp.where(i > 0, s_p, NEG_INF)
                m = jnp.maximum(m, jnp.max(s_p, axis=-1, keepdims=True))
            p_c = jnp.exp(s_c - m)
            den = jnp.sum(p_c, axis=-1, keepdims=True)
            num = jnp.dot(p_c.astype(_BF16), a_ref[0, r, pl.ds(r0, BAND), vc], preferred_element_type=_F32)
            if has_prev:
                p_p = jnp.exp(s_p - m)
                den = den + jnp.sum(p_p, axis=-1, keepdims=True)
                num = num + jnp.dot(p_p.astype(_BF16), a_ref[0, r, pl.ds(rp, BAND), vc],
                                    preferred_element_type=_F32)
            o_ref[0, h, out_rows, :] = num / den
            lse_ref[0, h, out_rows, :] = jnp.broadcast_to(m + jnp.log(den), (BAND, HEAD_DIM))

    if dil == 1:
        def qtile(i, carry):
            tile(0, i)
            return carry

        lax.fori_loop(0, nq, qtile, 0)
    else:
        for r in range(dil):
            for i in range(nq):
                tile(r, i)


def _mixer_a_group(tab, a_g, group, batch, seq):
    dil = DILATIONS[group]
    out_spec = pl.BlockSpec((1, HEADS_PER_GROUP, seq, HEAD_DIM), lambda b: (b, 0, 0, 0))
    out_sds = jax.ShapeDtypeStruct((batch, HEADS_PER_GROUP, seq, HEAD_DIM), _F32)
    return pl.pallas_call(
        functools.partial(_mixer_a_kernel, group=group, seq=seq),
        grid=(batch,),
        in_specs=[pl.BlockSpec(memory_space=pltpu.SMEM),
                  pl.BlockSpec((1, dil, seq // dil, 3 * GROUP_W), lambda b: (b, 0, 0, 0))],
        out_specs=[out_spec, out_spec],
        out_shape=[out_sds, out_sds],
        scratch_shapes=[pltpu.VMEM((HEADS_PER_GROUP, 2, BAND, BAND), _F32)],
        compiler_params=pltpu.CompilerParams(
            dimension_semantics=("arbitrary",), vmem_limit_bytes=V7X_VMEM_LIMIT),
        name=f"mixer_a_g{group}",
    )(tab, a_g)


def _mixer_b_kernel(tab_ref, lng_ref, lnb_ref, q_ref, k_ref, v_ref, qidx_ref, smallk_ref, smallq_ref,
                    o_ref, kidx_s, bias_s, key_s, selb_s, wrep_s, m_s, l_s, acc_s, *, tq, seq):
    b = pl.program_id(0)
    qi = pl.program_id(1)
    nkb = seq // tq
    lane_tiles = tq // 128
    scale = HEAD_DIM ** -0.5
    row = lax.broadcasted_iota(jnp.int32, (tq, tq), 0)
    col = lax.broadcasted_iota(jnp.int32, (tq, tq), 1)

    @pl.when((b == 0) & (qi == 0))
    def _build_bias():
        def per_delta(dl, carry):
            dist = row - col + dl * tq
            for h in range(HEADS_B):
                bias_s[dl, h] = _bias_from_distance(tab_ref, dist, HEADS_A + h)
            return carry

        lax.fori_loop(0, nkb, per_delta, 0)

    @pl.when(qi == 0)
    def _layer_norm_keys():
        def chunk(c, carry):
            r = pl.multiple_of(c * tq, tq)
            kx = smallk_ref[0, pl.ds(r, tq), :][:, :IDX_DIM]
            mu = jnp.mean(kx, axis=-1, keepdims=True)
            xc = kx - mu
            var = jnp.mean(xc * xc, axis=-1, keepdims=True)
            y = xc * lax.rsqrt(var + NORM_EPS) * lng_ref[...] + lnb_ref[...]
            z = jnp.zeros_like(y)
            kidx_s[0, pl.ds(r, tq), :] = jnp.concatenate([y, z], axis=-1).astype(_BF16)
            kidx_s[1, pl.ds(r, tq), :] = jnp.concatenate([z, y], axis=-1).astype(_BF16)
            return carry

        lax.fori_loop(0, nkb, chunk, 0)

    wq = smallq_ref[0][:, IDX_DIM:IDX_DIM + IDX_HEADS] * (IDX_HEADS ** -0.5 * IDX_DIM ** -0.5)
    for h in range(IDX_HEADS):
        wrep_s[h] = jnp.broadcast_to(wq[:, h:h + 1], (tq, 128))

    t_row = qi * tq + row

    def score_blk(kj, carry):
        r = pl.multiple_of(kj * tq, tq)
        acc = jnp.zeros((tq, tq), _F32)
        for h in range(IDX_HEADS):
            pair = qidx_ref[0][:, (h // 2) * 128:(h // 2 + 1) * 128]
            x = _dot_nt(pair, kidx_s[h % 2, pl.ds(r, tq), :])
            w = wrep_s[h]
            acc = acc + jnp.concatenate([w] * lane_tiles, axis=-1) * jnp.maximum(x, 0.0)
        bits = lax.bitcast_convert_type(acc, jnp.int32)
        key = bits ^ ((bits >> 31) & 0x7FFFFFFF)
        causal = (kj * tq + col) <= t_row
        key_s[kj] = jnp.where(causal, key, INT_MIN)
        return carry

    lax.fori_loop(0, qi + 1, score_blk, 0)

    rc_rows = 128

    def row_chunk(rc, carry):
        r = pl.multiple_of(rc * rc_rows, rc_rows)

        def count_ge(cand):
            def blk(kj, cnt):
                kk = key_s[kj, pl.ds(r, rc_rows), :]
                for c in range(lane_tiles):
                    cnt = cnt + jnp.where(kk[:, c * 128:(c + 1) * 128] >= cand, 1.0, 0.0)
                return cnt

            cnt = lax.fori_loop(0, qi + 1, blk, jnp.zeros((rc_rows, 128), _F32))
            return jnp.broadcast_to(jnp.sum(cnt, axis=-1, keepdims=True), (rc_rows, 128))

        zero = jnp.zeros((rc_rows, 128), jnp.int32)
        prefix = jnp.where(count_ge(zero) >= TOPK, zero, INT_MIN)

        def bit_body(bi, prefix):
            cand = prefix | lax.shift_left(jnp.int32(1), 30 - bi)
            return jnp.where(count_ge(cand) >= TOPK, cand, prefix)

        tau = lax.fori_loop(0, 31, bit_body, prefix)
        thr = jnp.maximum(tau, INT_MIN + 1)

        def write(kj, c2):
            kk = key_s[kj, pl.ds(r, rc_rows), :]
            selb_s[kj, pl.ds(r, rc_rows), :] = jnp.where(
                kk >= jnp.concatenate([thr] * lane_tiles, axis=-1), 0.0, NEG_INF)
            return c2

        lax.fori_loop(0, qi + 1, write, 0)
        return carry

    lax.fori_loop(0, tq // rc_rows, row_chunk, 0)

    m_s[...] = jnp.full(m_s.shape, -3e38, _F32)
    l_s[...] = jnp.zeros(l_s.shape, _F32)
    acc_s[...] = jnp.zeros(acc_s.shape, _F32)

    def attn_blk(kj, carry):
        r = pl.multiple_of(kj * tq, tq)
        selb = selb_s[kj]
        for h in range(HEADS_B):
            cs = slice(h * HEAD_DIM, (h + 1) * HEAD_DIM)
            s = _dot_nt(q_ref[0][:, cs], k_ref[0, pl.ds(r, tq), cs]) * scale + bias_s[qi - kj, h] + selb
            m_old = m_s[h]
            m_new = jnp.maximum(m_old, jnp.max(s, axis=-1, keepdims=True))
            alpha = jnp.exp(m_old - m_new)
            p = jnp.exp(s - m_new)
            l_s[h] = alpha * l_s[h] + jnp.sum(p, axis=-1, keepdims=True)
            acc_s[h] = alpha * acc_s[h] + jnp.dot(p.astype(_BF16), v_ref[0, pl.ds(r, tq), cs],
                                                  preferred_element_type=_F32)
            m_s[h] = m_new
        return carry

    lax.fori_loop(0, qi + 1, attn_blk, 0)
    for h in range(HEADS_B):
        o_ref[0, :, h * HEAD_DIM:(h + 1) * HEAD_DIM] = (acc_s[h] / l_s[h]).astype(o_ref.dtype)


def _mixer_b(tab, ln_g, ln_b, big3, small3, tq):
    batch, seq, _ = big3.shape
    nkb = seq // tq
    full = lambda cb: pl.BlockSpec((1, seq, WIDTH_B), lambda b, i, cb=cb: (b, 0, cb))
    tile = lambda cb: pl.BlockSpec((1, tq, WIDTH_B), lambda b, i, cb=cb: (b, i, cb))
    small_cb = small3.shape[-1] // SMALL_W - 1
    return pl.pallas_call(
        functools.partial(_mixer_b_kernel, tq=tq, seq=seq),
        grid=(batch, nkb),
        in_specs=[
            pl.BlockSpec(memory_space=pltpu.SMEM),
            pl.BlockSpec((1, IDX_DIM), lambda b, i: (0, 0)),
            pl.BlockSpec((1, IDX_DIM), lambda b, i: (0, 0)),
            tile(0), full(1), full(2), tile(3),
            pl.BlockSpec((1, seq, SMALL_W), lambda b, i: (b, 0, small_cb)),
            pl.BlockSpec((1, tq, SMALL_W), lambda b, i: (b, i, small_cb)),
        ],
        out_specs=pl.BlockSpec((1, tq, WIDTH_B), lambda b, i: (b, i, 0)),
        out_shape=jax.ShapeDtypeStruct((batch, seq, WIDTH_B), _BF16),
        scratch_shapes=[
            pltpu.VMEM((2, seq, 2 * IDX_DIM), _BF16),
            pltpu.VMEM((nkb, HEADS_B, tq, tq), _F32),
            pltpu.VMEM((nkb, tq, tq), jnp.int32),
            pltpu.VMEM((nkb, tq, tq), _F32),
            pltpu.VMEM((IDX_HEADS, tq, 128), _F32),
            pltpu.VMEM((HEADS_B, tq, 1), _F32),
            pltpu.VMEM((HEADS_B, tq, 1), _F32),
            pltpu.VMEM((HEADS_B, tq, HEAD_DIM), _F32),
        ],
        compiler_params=pltpu.CompilerParams(
            dimension_semantics=("arbitrary", "arbitrary"), vmem_limit_bytes=V7X_VMEM_LIMIT),
        name="mixer_b",
    )(tab, ln_g, ln_b, big3, big3, big3, big3, small3, small3)


def _merge_kernel(x_ref, oa0_ref, oa1_ref, oa2_ref, l0_ref, l1_ref, l2_ref, ob_ref, ga_ref, gb_ref,
                  wpa_ref, wpb_ref, wout_ref, y_ref):
    heads = []
    for h in range(HEADS_PER_GROUP):
        l0, l1, l2 = l0_ref[0, h], l1_ref[0, h], l2_ref[0, h]
        m = jnp.maximum(jnp.maximum(l0, l1), l2)
        e0, e1, e2 = jnp.exp(l0 - m), jnp.exp(l1 - m), jnp.exp(l2 - m)
        heads.append((e0 * oa0_ref[0, h] + e1 * oa1_ref[0, h] + e2 * oa2_ref[0, h]) / (e0 + e1 + e2))
    o_a = jnp.concatenate(heads, axis=-1)
    pa = jnp.dot(o_a.astype(_BF16), wpa_ref[...], preferred_element_type=_F32)
    pb = jnp.dot(ob_ref[...], wpb_ref[...], preferred_element_type=_F32)
    merged = jax.nn.sigmoid(ga_ref[...]) * pa + jax.nn.sigmoid(gb_ref[...]) * pb
    y_ref[...] = x_ref[...] + jnp.dot(merged.astype(_BF16), wout_ref[...], preferred_element_type=_F32)


def _merge(x2d, oa, lse, ob2d, fout, wpa, wpb, wout, tm):
    n, d = x2d.shape
    tiles_per_seq = oa[0].shape[2] // tm
    rows = lambda w, cb=0: pl.BlockSpec((tm, w), lambda i, cb=cb: (i, cb))
    heads = pl.BlockSpec((1, HEADS_PER_GROUP, tm, HEAD_DIM),
                         lambda i: (i // tiles_per_seq, 0, i % tiles_per_seq, 0))
    const = lambda shape: pl.BlockSpec(shape, lambda i: (0, 0), pipeline_mode=pl.Buffered(1))
    return pl.pallas_call(
        _merge_kernel,
        grid=(n // tm,),
        in_specs=[rows(d), heads, heads, heads, heads, heads, heads, rows(WIDTH_B), rows(d, 0), rows(d, 1),
                  const(wpa.shape), const(wpb.shape), const(wout.shape)],
        out_specs=rows(d),
        out_shape=jax.ShapeDtypeStruct((n, d), _F32),
        compiler_params=pltpu.CompilerParams(
            dimension_semantics=("arbitrary",), vmem_limit_bytes=V7X_VMEM_LIMIT),
        name="merge_out_proj",
    )(x2d, oa[0], oa[1], oa[2], lse[0], lse[1], lse[2], ob2d, fout, fout, wpa, wpb, wout)


def _mlp_kernel(x_ref, g_ref, gf_ref, wu_ref, wd_ref, y_ref, h_ref, acc_ref):
    j = pl.program_id(1)

    @pl.when(j == 0)
    def _():
        x = x_ref[...]
        ms = jnp.mean(x * x, axis=-1, keepdims=True)
        h_ref[...] = (x * lax.rsqrt(ms + NORM_EPS) * g_ref[...]).astype(_BF16)
        acc_ref[...] = jnp.zeros_like(acc_ref)

    u = jnp.maximum(jnp.dot(h_ref[...], wu_ref[...], preferred_element_type=_F32), 0.0)
    acc_ref[...] += jnp.dot((u * u).astype(_BF16), wd_ref[...], preferred_element_type=_F32)

    @pl.when(j == pl.num_programs(1) - 1)
    def _():
        y = x_ref[...] + acc_ref[...]
        ms = jnp.mean(y * y, axis=-1, keepdims=True)
        y_ref[...] = y * lax.rsqrt(ms + NORM_EPS) * gf_ref[...]


def _mlp(x2d, g, gf, wu, wd, tm, tf):
    n, d = x2d.shape
    dff = wu.shape[1]
    return pl.pallas_call(
        _mlp_kernel,
        grid=(n // tm, dff // tf),
        in_specs=[
            pl.BlockSpec((tm, d), lambda i, j: (i, 0)),
            pl.BlockSpec((1, d), lambda i, j: (0, 0)),
            pl.BlockSpec((1, d), lambda i, j: (0, 0)),
            pl.BlockSpec((d, tf), lambda i, j: (0, j)),
            pl.BlockSpec((tf, d), lambda i, j: (j, 0)),
        ],
        out_specs=pl.BlockSpec((tm, d), lambda i, j: (i, 0)),
        out_shape=jax.ShapeDtypeStruct((n, d), _F32),
        scratch_shapes=[pltpu.VMEM((tm, d), _BF16), pltpu.VMEM((tm, d), _F32)],
        compiler_params=pltpu.CompilerParams(
            dimension_semantics=("arbitrary", "arbitrary"), vmem_limit_bytes=V7X_VMEM_LIMIT),
        name="mlp_final_norm",
    )(x2d, g, gf, wu, wd)


def _largest_tile(n, limit, step):
    t = (min(n, limit) // step) * step
    while n % t:
        t -= step
    return t


def kernel(x, norm_mix_g, w_in, idx_k_norm_g, idx_k_norm_b, rel_bias_table, w_proj_a, w_proj_b, w_out,
           norm_mlp_g, w_mlp_up, w_mlp_down, norm_final_g):
    batch, seq, d = x.shape
    assert seq == MAX_DISTANCE and seq % (DILATIONS[-1] * BAND) == 0
    assert w_in.shape[0] == 1, "single layer"
    n = batch * seq
    x2d = x.reshape(n, d)
    tab = rel_bias_table.reshape(-1)

    w = w_in[0]
    a0, b0 = 0, 3 * WIDTH_A
    q0 = b0 + 3 * WIDTH_B
    k0 = q0 + IDX_W
    wi0 = k0 + IDX_DIM
    ga0 = wi0 + IDX_HEADS
    w_b = w[:, b0:k0].astype(_BF16)
    w_f32out = jnp.concatenate(
        [w[:, ga0:ga0 + 2 * d], w[:, k0:ga0], jnp.zeros((d, SMALL_W - IDX_DIM - IDX_HEADS), w.dtype)],
        axis=1).astype(_BF16)
    w_a = w[:, a0:b0].reshape(d, 3, N_GROUPS, GROUP_W)

    h = _rmsnorm(x2d, norm_mix_g[0].reshape(1, d), _largest_tile(n, 256, 8))
    tm = _largest_tile(seq, 1024, 256)
    big_b = _matmul(h, w_b, _BF16, tm, _largest_tile(w_b.shape[1], 512, 128), "proj_b_idx")
    fout = _matmul(h, w_f32out, _F32, tm, _largest_tile(w_f32out.shape[1], 1408, 128), "proj_gates")

    oa, lse = [], []
    for group in range(N_GROUPS):
        w_g = w_a[:, :, group, :].reshape(d, 3 * GROUP_W).astype(_BF16)
        a_g = _matmul_classmajor(h, w_g, DILATIONS[group], batch, seq, tm, GROUP_W, f"proj_a_g{group}")
        o_g, lse_g = _mixer_a_group(tab, a_g, group, batch, seq)
        oa.append(o_g)
        lse.append(lse_g)

    ob = _mixer_b(tab, idx_k_norm_g[0].reshape(1, IDX_DIM), idx_k_norm_b[0].reshape(1, IDX_DIM),
                  big_b.reshape(batch, seq, -1), fout.reshape(batch, seq, -1), tq=256)

    x1 = _merge(x2d, oa, lse, ob.reshape(n, WIDTH_B), fout,
                w_proj_a[0].astype(_BF16), w_proj_b[0].astype(_BF16), w_out[0].astype(_BF16),
                tm=_largest_tile(n, 256, 128))
    y = _mlp(x1, norm_mlp_g[0].reshape(1, d), norm_final_g.reshape(1, d),
             w_mlp_up[0].astype(_BF16), w_mlp_down[0].astype(_BF16),
             tm=_largest_tile(n, 512, 128), tf=_largest_tile(w_mlp_up.shape[-1], 512, 128))
    return y.reshape(batch, seq, d)
```

```python
import functools
import math

import jax
import jax.numpy as jnp
import numpy as np
from jax import lax
from jax.experimental import pallas as pl
from jax.experimental.pallas import tpu as pltpu

HEAD_DIM = 128
DILATIONS = (1, 4, 16)
N_GROUPS = 3
HEADS_PER_GROUP = 4
HEADS_A = N_GROUPS * HEADS_PER_GROUP
BAND = 128
HEADS_B = 8
N_HEADS = HEADS_A + HEADS_B
IDX_HEADS = 16
IDX_DIM = 64
TOPK = 256
N_BUCKETS = 32
MAX_DISTANCE = 2048
NORM_EPS = 1e-6
NEG_INF = -1e30
INT_MIN = -(2**31)

WIDTH_A = HEADS_A * HEAD_DIM
WIDTH_B = HEADS_B * HEAD_DIM
GROUP_W = HEADS_PER_GROUP * HEAD_DIM
IDX_W = IDX_HEADS * IDX_DIM
SMALL_W = 128
MERGE_TILE = 256

V7X_VMEM_LIMIT = 56 * 1024 * 1024

_F32 = jnp.float32
_BF16 = jnp.bfloat16


def _bucket_bounds():
    n = np.arange(MAX_DISTANCE, dtype=np.int32)
    max_exact = N_BUCKETS // 2
    nf = np.maximum(n, 1).astype(np.float32)
    large = max_exact + (
        np.log(nf / np.float32(max_exact)) / np.float32(math.log(MAX_DISTANCE / max_exact))
        * np.float32(N_BUCKETS - max_exact)
    ).astype(np.int32)
    large = np.minimum(large, N_BUCKETS - 1)
    bucket = np.where(n < max_exact, n, large)
    assert np.all(np.diff(bucket) >= 0)
    bounds = [int(np.argmax(bucket >= b)) if np.any(bucket >= b) else MAX_DISTANCE for b in range(N_BUCKETS)]
    return tuple(bounds)


_BOUNDS = _bucket_bounds()


def _dot_nt(a, b):
    return lax.dot_general(a, b, (((1,), (1,)), ((), ())), preferred_element_type=_F32)


def _bias_from_distance(tab_ref, dist, head):
    val = jnp.full(dist.shape, tab_ref[head], _F32)
    for bkt in range(1, N_BUCKETS):
        val = jnp.where(dist >= _BOUNDS[bkt], tab_ref[bkt * N_HEADS + head], val)
    return val


def _rmsnorm_kernel(x_ref, g_ref, h_ref):
    x = x_ref[...]
    ms = jnp.mean(x * x, axis=-1, keepdims=True)
    h_ref[...] = (x * lax.rsqrt(ms + NORM_EPS) * g_ref[...]).astype(h_ref.dtype)


def _rmsnorm(x2d, g, tm):
    n, d = x2d.shape
    return pl.pallas_call(
        _rmsnorm_kernel,
        grid=(n // tm,),
        in_specs=[pl.BlockSpec((tm, d), lambda i: (i, 0)), pl.BlockSpec((1, d), lambda i: (0, 0))],
        out_specs=pl.BlockSpec((tm, d), lambda i: (i, 0)),
        out_shape=jax.ShapeDtypeStruct((n, d), _BF16),
        compiler_params=pltpu.CompilerParams(
            dimension_semantics=("arbitrary",), vmem_limit_bytes=V7X_VMEM_LIMIT),
        name="rmsnorm_mix",
    )(x2d, g)


def _matmul_kernel(h_ref, w_ref, o_ref):
    o_ref[...] = jnp.dot(h_ref[...], w_ref[...], preferred_element_type=_F32).astype(o_ref.dtype)


def _matmul(h, w, out_dtype, tm, tn, name):
    n, d = h.shape
    cols = w.shape[1]
    return pl.pallas_call(
        _matmul_kernel,
        grid=(n // tm, cols // tn),
        in_specs=[pl.BlockSpec((tm, d), lambda i, j: (i, 0)), pl.BlockSpec((d, tn), lambda i, j: (0, j))],
        out_specs=pl.BlockSpec((tm, tn), lambda i, j: (i, j)),
        out_shape=jax.ShapeDtypeStruct((n, cols), out_dtype),
        compiler_params=pltpu.CompilerParams(
            dimension_semantics=("arbitrary", "arbitrary"), vmem_limit_bytes=V7X_VMEM_LIMIT),
        name=name,
    )(h, w)


def _matmul_classmajor_kernel(h_ref, w_ref, o_ref, acc_ref, *, dil, rows):
    res = jnp.dot(h_ref[...], w_ref[...], preferred_element_type=_F32)
    if dil == 1:
        o_ref[0, 0] = res.astype(o_ref.dtype)
        return
    for c in range(acc_ref.shape[0]):
        acc_ref[c] = res[:, c * 128:(c + 1) * 128]
    for c in range(acc_ref.shape[0]):
        for r in range(dil):
            o_ref[0, r, :, c * 128:(c + 1) * 128] = (
                acc_ref[c, pl.ds(r, rows, stride=dil), :].astype(o_ref.dtype))


def _matmul_classmajor(h, w, dil, batch, seq, tm, tn, name):
    n, d = h.shape
    cols = w.shape[1]
    tiles_per_seq = seq // tm
    rows = tm // dil
    return pl.pallas_call(
        functools.partial(_matmul_classmajor_kernel, dil=dil, rows=rows),
        grid=(n // tm, cols // tn),
        in_specs=[pl.BlockSpec((tm, d), lambda i, j: (i, 0)), pl.BlockSpec((d, tn), lambda i, j: (0, j))],
        out_specs=pl.BlockSpec((1, dil, rows, tn),
                               lambda i, j: (i // tiles_per_seq, 0, i % tiles_per_seq, j)),
        out_shape=jax.ShapeDtypeStruct((batch, dil, seq // dil, cols), _BF16),
        scratch_shapes=[pltpu.VMEM((tn // 128, tm, 128), _F32)],
        compiler_params=pltpu.CompilerParams(
            dimension_semantics=("arbitrary", "arbitrary"), vmem_limit_bytes=V7X_VMEM_LIMIT),
        name=name,
    )(h, w)


def _tile_class_major(dil):
    return dil % 8 == 0


def _mixer_a_kernel(tab_ref, a_ref, o_ref, lse_ref, bias_ref, *, group, seq):
    dil = DILATIONS[group]
    nq = seq // dil // BAND
    scale = HEAD_DIM ** -0.5

    @pl.when(pl.program_id(0) == 0)
    def _build_bias():
        row = lax.broadcasted_iota(jnp.int32, (BAND, BAND), 0)
        col = lax.broadcasted_iota(jnp.int32, (BAND, BAND), 1)
        for side in range(2):
            j = row - col + BAND * (1 - side)
            valid = jnp.where(j >= 0, j, BAND + 1) <= BAND
            for h in range(HEADS_PER_GROUP):
                val = _bias_from_distance(tab_ref, j * dil, group * HEADS_PER_GROUP + h)
                bias_ref[h, :, side * BAND:(side + 1) * BAND] = jnp.where(valid, val, NEG_INF)

    def key_rows(i):
        return pl.ds(0, BAND) if i == 0 else pl.ds((i - 1) * BAND, 2 * BAND)

    def logits(r, i, h):
        qc = slice(h * HEAD_DIM, (h + 1) * HEAD_DIM)
        kc = slice(GROUP_W + h * HEAD_DIM, GROUP_W + (h + 1) * HEAD_DIM)
        bias = bias_ref[h, :, BAND:] if i == 0 else bias_ref[h]
        return _dot_nt(a_ref[0, r, pl.ds(i * BAND, BAND), qc], a_ref[0, r, key_rows(i), kc]) * scale + bias

    def finish(r, i, h, s):
        m = jnp.max(s, axis=-1, keepdims=True)
        p = jnp.exp(s - m).astype(_BF16)
        v = a_ref[0, r, key_rows(i), 2 * GROUP_W + h * HEAD_DIM:2 * GROUP_W + (h + 1) * HEAD_DIM]
        res = jnp.dot(p, jnp.concatenate([v, jnp.ones_like(v)], axis=-1), preferred_element_type=_F32)
        den = res[:, HEAD_DIM:]
        o_val = res[:, :HEAD_DIM] / den
        lse_val = m + jnp.log(den)
        if not _tile_class_major(dil):
            out_rows = pl.ds(i * BAND, BAND) if dil == 1 else pl.ds(r + i * BAND * dil, BAND, stride=dil)
            o_ref[0, h, out_rows, :] = o_val
            lse_ref[0, h, out_rows, :] = lse_val
        else:
            per = MERGE_TILE // dil
            for c in range(BAND // per):
                rows = pl.ds((i * (BAND // per) + c) * MERGE_TILE + r * per, per)
                o_ref[0, h, rows, :] = o_val[c * per:(c + 1) * per]
                lse_ref[0, h, rows, :] = lse_val[c * per:(c + 1) * per]

    pending = None
    for r in range(dil):
        for i in range(nq):
            cur = [(r, i, h, logits(r, i, h)) for h in range(HEADS_PER_GROUP)]
            if pending is not None:
                for unit in pending:
                    finish(*unit)
            pending = cur
    for unit in pending:
        finish(*unit)


def _mixer_a_group(tab, a_g, group, batch, seq):
    dil = DILATIONS[group]
    out_spec = pl.BlockSpec((1, HEADS_PER_GROUP, seq, HEAD_DIM), lambda b: (b, 0, 0, 0))
    out_sds = jax.ShapeDtypeStruct((batch, HEADS_PER_GROUP, seq, HEAD_DIM), _F32)
    return pl.pallas_call(
        functools.partial(_mixer_a_kernel, group=group, seq=seq),
        grid=(batch,),
        in_specs=[pl.BlockSpec(memory_space=pltpu.SMEM),
                  pl.BlockSpec((1, dil, seq // dil, 3 * GROUP_W), lambda b: (b, 0, 0, 0))],
        out_specs=[out_spec, out_spec],
        out_shape=[out_sds, out_sds],
        scratch_shapes=[pltpu.VMEM((HEADS_PER_GROUP, BAND, 2 * BAND), _F32)],
        compiler_params=pltpu.CompilerParams(
            dimension_semantics=("arbitrary",), vmem_limit_bytes=V7X_VMEM_LIMIT),
        name=f"mixer_a_g{group}",
    )(tab, a_g)


def _mixer_b_kernel(tab_ref, lng_ref, lnb_ref, q_ref, k_ref, v_ref, qidx_ref, smallk_ref, smallq_ref,
                    o_ref, kidx_s, bias_s, key_s, selb_s, wt_s, s_s, mx_s, mrep_s, lp_s, acc_s, *, tq, seq):
    b = pl.program_id(0)
    qi = pl.program_id(1)
    nkb = seq // tq
    lane_tiles = tq // 128
    scale = HEAD_DIM ** -0.5
    inv_scale = HEAD_DIM ** 0.5
    neg_scaled = NEG_INF * inv_scale
    row = lax.broadcasted_iota(jnp.int32, (tq, tq), 0)
    col = lax.broadcasted_iota(jnp.int32, (tq, tq), 1)

    @pl.when((b == 0) & (qi == 0))
    def _build_bias():
        def per_delta(dl, carry):
            dist = row - col + dl * tq
            for h in range(HEADS_B):
                bias_s[dl, h] = _bias_from_distance(tab_ref, dist, HEADS_A + h) * inv_scale
            return carry

        lax.fori_loop(0, nkb, per_delta, 0)

    @pl.when(qi == 0)
    def _layer_norm_keys():
        def chunk(c, carry):
            r = pl.multiple_of(c * tq, tq)
            kx = smallk_ref[0, pl.ds(r, tq), :][:, :IDX_DIM]
            mu = jnp.mean(kx, axis=-1, keepdims=True)
            xc = kx - mu
            var = jnp.mean(xc * xc, axis=-1, keepdims=True)
            y = xc * lax.rsqrt(var + NORM_EPS) * lng_ref[...] + lnb_ref[...]
            z = jnp.zeros_like(y)
            kidx_s[0, pl.ds(r, tq), :] = jnp.concatenate([y, z], axis=-1).astype(_BF16)
            kidx_s[1, pl.ds(r, tq), :] = jnp.concatenate([z, y], axis=-1).astype(_BF16)
            return carry

        lax.fori_loop(0, nkb, chunk, 0)

    wt_s[...] = (smallq_ref[0] * (IDX_HEADS ** -0.5 * IDX_DIM ** -0.5)).T

    def score_blk(kj, carry):
        r = pl.multiple_of(kj * tq, tq)
        acc = jnp.zeros((tq, tq), _F32)
        for h in range(IDX_HEADS):
            pair = qidx_ref[0][:, (h // 2) * 128:(h // 2 + 1) * 128]
            x = _dot_nt(kidx_s[h % 2, pl.ds(r, tq), :], pair)
            acc = acc + wt_s[IDX_DIM + h:IDX_DIM + h + 1, :] * jnp.maximum(x, 0.0)
        bits = lax.bitcast_convert_type(acc, jnp.int32)
        key = bits ^ ((bits >> 31) & 0x7FFFFFFF)
        causal = (kj * tq + row) <= (qi * tq + col)
        key_s[kj] = jnp.where(causal, key, INT_MIN)
        return carry

    lax.fori_loop(0, qi + 1, score_blk, 0)

    def count_ge(cand):
        def blk(kj, cnt):
            kk = key_s[kj].reshape(tq // 8, 8, tq)
            return cnt + jnp.sum(jnp.where(kk >= cand[None], 1.0, 0.0), axis=0)

        cnt = lax.fori_loop(0, qi + 1, blk, jnp.zeros((8, tq), _F32))
        return jnp.broadcast_to(jnp.sum(cnt, axis=0, keepdims=True), (8, tq))

    zero = jnp.zeros((8, tq), jnp.int32)
    prefix = jnp.where(count_ge(zero) >= TOPK, zero, INT_MIN)

    def bit_body(bi, prefix):
        cand = prefix | lax.shift_left(jnp.int32(1), 30 - bi)
        return jnp.where(count_ge(cand) >= TOPK, cand, prefix)

    tau = lax.fori_loop(0, 31, bit_body, prefix)
    thr = jnp.maximum(tau, INT_MIN + 1)

    def write_sel(kj, carry):
        kk = key_s[kj].reshape(tq // 8, 8, tq)
        sel_t = jnp.where(kk >= thr[None], 0.0, neg_scaled).reshape(tq, tq)
        selb_s[kj] = sel_t.T
        return carry

    lax.fori_loop(0, qi + 1, write_sel, 0)

    head_group = s_s.shape[0]
    exp2_coeff = scale * math.log2(math.e)
    for h0 in range(0, HEADS_B, head_group):
        for hl in range(head_group):
            mx_s[hl] = jnp.full((tq, 128), -3e38, _F32)

        def logits_blk(kj, carry, h0=h0):
            r = pl.multiple_of(kj * tq, tq)
            for hl in range(head_group):
                h = h0 + hl
                cs = slice(h * HEAD_DIM, (h + 1) * HEAD_DIM)
                s = _dot_nt(q_ref[0][:, cs], k_ref[0, pl.ds(r, tq), cs]) + bias_s[qi - kj, h] + selb_s[kj]
                s_s[hl, kj] = s
                part = s[:, :128]
                for c in range(1, lane_tiles):
                    part = jnp.maximum(part, s[:, c * 128:(c + 1) * 128])
                mx_s[hl] = jnp.maximum(mx_s[hl], part)
            return carry

        lax.fori_loop(0, qi + 1, logits_blk, 0)
        for hl in range(head_group):
            m = jnp.max(mx_s[hl], axis=-1, keepdims=True)
            mrep_s[hl] = jnp.broadcast_to(m, (tq, 128))
            lp_s[hl] = jnp.zeros((tq, 128), _F32)
            acc_s[hl] = jnp.zeros((tq, HEAD_DIM), _F32)

        def pv_blk(kj, carry, h0=h0):
            r = pl.multiple_of(kj * tq, tq)
            for hl in range(head_group):
                h = h0 + hl
                cs = slice(h * HEAD_DIM, (h + 1) * HEAD_DIM)
                m = mrep_s[hl]
                p = jnp.exp2((s_s[hl, kj] - jnp.concatenate([m] * lane_tiles, axis=-1)) * exp2_coeff)
                part = p[:, :128]
                for c in range(1, lane_tiles):
                    part = part + p[:, c * 128:(c + 1) * 128]
                lp_s[hl] = lp_s[hl] + part
                acc_s[hl] = acc_s[hl] + jnp.dot(p.astype(_BF16), v_ref[0, pl.ds(r, tq), cs],
                                                preferred_element_type=_F32)
            return carry

        lax.fori_loop(0, qi + 1, pv_blk, 0)
        for hl in range(head_group):
            h = h0 + hl
            den = jnp.sum(lp_s[hl], axis=-1, keepdims=True)
            o_ref[0, :, h * HEAD_DIM:(h + 1) * HEAD_DIM] = (acc_s[hl] / den).astype(o_ref.dtype)


def _mixer_b(tab, ln_g, ln_b, big3, small3, tq):
    batch, seq, _ = big3.shape
    nkb = seq // tq
    head_group = 4
    full = lambda cb: pl.BlockSpec((1, seq, WIDTH_B), lambda b, i, cb=cb: (b, 0, cb),
                                   pipeline_mode=pl.Buffered(1))
    tile = lambda cb: pl.BlockSpec((1, tq, WIDTH_B), lambda b, i, cb=cb: (b, i, cb))
    small_cb = small3.shape[-1] // SMALL_W - 1
    return pl.pallas_call(
        functools.partial(_mixer_b_kernel, tq=tq, seq=seq),
        grid=(batch, nkb),
        in_specs=[
            pl.BlockSpec(memory_space=pltpu.SMEM),
            pl.BlockSpec((1, IDX_DIM), lambda b, i: (0, 0)),
            pl.BlockSpec((1, IDX_DIM), lambda b, i: (0, 0)),
            tile(0), full(1), full(2), tile(3),
            pl.BlockSpec((1, seq, SMALL_W), lambda b, i: (b, 0, small_cb)),
            pl.BlockSpec((1, tq, SMALL_W), lambda b, i: (b, i, small_cb)),
        ],
        out_specs=pl.BlockSpec((1, tq, WIDTH_B), lambda b, i: (b, i, 0)),
        out_shape=jax.ShapeDtypeStruct((batch, seq, WIDTH_B), _BF16),
        scratch_shapes=[
            pltpu.VMEM((2, seq, 2 * IDX_DIM), _BF16),
            pltpu.VMEM((nkb, HEADS_B, tq, tq), _F32),
            pltpu.VMEM((nkb, tq, tq), jnp.int32),
            pltpu.VMEM((nkb, tq, tq), _F32),
            pltpu.VMEM((SMALL_W, tq), _F32),
            pltpu.VMEM((head_group, nkb, tq, tq), _F32),
            pltpu.VMEM((head_group, tq, 128), _F32),
            pltpu.VMEM((head_group, tq, 128), _F32),
            pltpu.VMEM((head_group, tq, 128), _F32),
            pltpu.VMEM((head_group, tq, HEAD_DIM), _F32),
        ],
        compiler_params=pltpu.CompilerParams(
            dimension_semantics=("arbitrary", "arbitrary"), vmem_limit_bytes=V7X_VMEM_LIMIT),
        name="mixer_b",
    )(tab, ln_g, ln_b, big3, big3, big3, big3, small3, small3)


def _merge_kernel(x_ref, oa0_ref, oa1_ref, oa2_ref, l0_ref, l1_ref, l2_ref, ob_ref, ga_ref, gb_ref,
                  wpa_ref, wpb_ref, wout_ref, y_ref):
    def natural(ref, h, group):
        dil = DILATIONS[group]
        if not _tile_class_major(dil):
            return ref[0, h]
        per = MERGE_TILE // dil
        return jnp.concatenate([ref[0, h, pl.ds(slot, dil, stride=per), :] for slot in range(per)], axis=0)

    heads = []
    for h in range(HEADS_PER_GROUP):
        l0, l1, l2 = natural(l0_ref, h, 0), natural(l1_ref, h, 1), natural(l2_ref, h, 2)
        m = jnp.maximum(jnp.maximum(l0, l1), l2)
        e0, e1, e2 = jnp.exp(l0 - m), jnp.exp(l1 - m), jnp.exp(l2 - m)
        heads.append((e0 * natural(oa0_ref, h, 0) + e1 * natural(oa1_ref, h, 1) + e2 * natural(oa2_ref, h, 2))
                     / (e0 + e1 + e2))
    o_a = jnp.concatenate(heads, axis=-1)
    pa = jnp.dot(o_a.astype(_BF16), wpa_ref[...], preferred_element_type=_F32)
    pb = jnp.dot(ob_ref[...], wpb_ref[...], preferred_element_type=_F32)
    merged = jax.nn.sigmoid(ga_ref[...]) * pa + jax.nn.sigmoid(gb_ref[...]) * pb
    y_ref[...] = x_ref[...] + jnp.dot(merged.astype(_BF16), wout_ref[...], preferred_element_type=_F32)


def _merge(x2d, oa, lse, ob2d, fout, wpa, wpb, wout, tm):
    n, d = x2d.shape
    tiles_per_seq = oa[0].shape[2] // tm
    rows = lambda w, cb=0: pl.BlockSpec((tm, w), lambda i, cb=cb: (i, cb))
    heads = pl.BlockSpec((1, HEADS_PER_GROUP, tm, HEAD_DIM),
                         lambda i: (i // tiles_per_seq, 0, i % tiles_per_seq, 0))
    const = lambda shape: pl.BlockSpec(shape, lambda i: (0, 0), pipeline_mode=pl.Buffered(1))
    return pl.pallas_call(
        _merge_kernel,
        grid=(n // tm,),
        in_specs=[rows(d), heads, heads, heads, heads, heads, heads, rows(WIDTH_B), rows(d, 0), rows(d, 1),
                  const(wpa.shape), const(wpb.shape), const(wout.shape)],
        out_specs=rows(d),
        out_shape=jax.ShapeDtypeStruct((n, d), _F32),
        compiler_params=pltpu.CompilerParams(
            dimension_semantics=("arbitrary",), vmem_limit_bytes=V7X_VMEM_LIMIT),
        name="merge_out_proj",
    )(x2d, oa[0], oa[1], oa[2], lse[0], lse[1], lse[2], ob2d, fout, fout, wpa, wpb, wout)


def _mlp_kernel(x_ref, g_ref, gf_ref, wu_ref, wd_ref, y_ref, h_ref, acc_ref):
    j = pl.program_id(1)

    @pl.when(j == 0)
    def _():
        x = x_ref[...]
        ms = jnp.mean(x * x, axis=-1, keepdims=True)
        h_ref[...] = (x * lax.rsqrt(ms + NORM_EPS) * g_ref[...]).astype(_BF16)
        acc_ref[...] = jnp.zeros_like(acc_ref)

    u = jnp.maximum(jnp.dot(h_ref[...], wu_ref[...], preferred_element_type=_F32), 0.0)
    acc_ref[...] += jnp.dot((u * u).astype(_BF16), wd_ref[...], preferred_element_type=_F32)

    @pl.when(j == pl.num_programs(1) - 1)
    def _():
        y = x_ref[...] + acc_ref[...]
        ms = jnp.mean(y * y, axis=-1, keepdims=True)
        y_ref[...] = y * lax.rsqrt(ms + NORM_EPS) * gf_ref[...]


def _mlp(x2d, g, gf, wu, wd, tm, tf):
    n, d = x2d.shape
    dff = wu.shape[1]
    return pl.pallas_call(
        _mlp_kernel,
        grid=(n // tm, dff // tf),
        in_specs=[
            pl.BlockSpec((tm, d), lambda i, j: (i, 0)),
            pl.BlockSpec((1, d), lambda i, j: (0, 0)),
            pl.BlockSpec((1, d), lambda i, j: (0, 0)),
            pl.BlockSpec((d, tf), lambda i, j: (0, j)),
            pl.BlockSpec((tf, d), lambda i, j: (j, 0)),
        ],
        out_specs=pl.BlockSpec((tm, d), lambda i, j: (i, 0)),
        out_shape=jax.ShapeDtypeStruct((n, d), _F32),
        scratch_shapes=[pltpu.VMEM((tm, d), _BF16), pltpu.VMEM((tm, d), _F32)],
        compiler_params=pltpu.CompilerParams(
            dimension_semantics=("arbitrary", "arbitrary"), vmem_limit_bytes=V7X_VMEM_LIMIT),
        name="mlp_final_norm",
    )(x2d, g, gf, wu, wd)


def _largest_tile(n, limit, step):
    t = (min(n, limit) // step) * step
    while n % t:
        t -= step
    return t


def kernel(x, norm_mix_g, w_in, idx_k_norm_g, idx_k_norm_b, rel_bias_table, w_proj_a, w_proj_b, w_out,
           norm_mlp_g, w_mlp_up, w_mlp_down, norm_final_g):
    batch, seq, d = x.shape
    assert seq == MAX_DISTANCE and seq % (DILATIONS[-1] * BAND) == 0
    assert w_in.shape[0] == 1, "single layer"
    n = batch * seq
    x2d = x.reshape(n, d)
    tab = rel_bias_table.reshape(-1)

    w = w_in[0]
    a0, b0 = 0, 3 * WIDTH_A
    q0 = b0 + 3 * WIDTH_B
    k0 = q0 + IDX_W
    wi0 = k0 + IDX_DIM
    ga0 = wi0 + IDX_HEADS
    w_b = w[:, b0:k0].astype(_BF16)
    w_f32out = jnp.concatenate(
        [w[:, ga0:ga0 + 2 * d], w[:, k0:ga0], jnp.zeros((d, SMALL_W - IDX_DIM - IDX_HEADS), w.dtype)],
        axis=1).astype(_BF16)
    w_a = w[:, a0:b0].reshape(d, 3, N_GROUPS, GROUP_W)

    h = _rmsnorm(x2d, norm_mix_g[0].reshape(1, d), _largest_tile(n, 256, 8))
    tm = _largest_tile(seq, 1024, 256)
    big_b = _matmul(h, w_b, _BF16, tm, _largest_tile(w_b.shape[1], 512, 128), "proj_b_idx")
    fout = _matmul(h, w_f32out, _F32, tm, _largest_tile(w_f32out.shape[1], 1408, 128), "proj_gates")

    oa, lse = [], []
    for group in range(N_GROUPS):
        w_g = w_a[:, :, group, :].reshape(d, 3 * GROUP_W).astype(_BF16)
        a_g = _matmul_classmajor(h, w_g, DILATIONS[group], batch, seq, tm, GROUP_W, f"proj_a_g{group}")
        o_g, lse_g = _mixer_a_group(tab, a_g, group, batch, seq)
        oa.append(o_g)
        lse.append(lse_g)

    ob = _mixer_b(tab, idx_k_norm_g[0].reshape(1, IDX_DIM), idx_k_norm_b[0].reshape(1, IDX_DIM),
                  big_b.reshape(batch, seq, -1), fout.reshape(batch, seq, -1), tq=256)

    x1 = _merge(x2d, oa, lse, ob.reshape(n, WIDTH_B), fout,
                w_proj_a[0].astype(_BF16), w_proj_b[0].astype(_BF16), w_out[0].astype(_BF16),
                tm=MERGE_TILE)
    y = _mlp(x1, norm_mlp_g[0].reshape(1, d), norm_final_g.reshape(1, d),
             w_mlp_up[0].astype(_BF16), w_mlp_down[0].astype(_BF16),
             tm=_largest_tile(n, 512, 128), tf=_largest_tile(w_mlp_up.shape[-1], 512, 128))
    return y.reshape(batch, seq, d)
```

```python
import functools
import math

import jax
import jax.numpy as jnp
import numpy as np
from jax import lax
from jax.experimental import pallas as pl
from jax.experimental.pallas import tpu as pltpu

HEAD_DIM = 128
DILATIONS = (1, 4, 16)
N_GROUPS = 3
HEADS_PER_GROUP = 4
HEADS_A = N_GROUPS * HEADS_PER_GROUP
BAND = 128
HEADS_B = 8
N_HEADS = HEADS_A + HEADS_B
IDX_HEADS = 16
IDX_DIM = 64
TOPK = 256
N_BUCKETS = 32
MAX_DISTANCE = 2048
NORM_EPS = 1e-6
NEG_INF = -1e30
INT_MIN = -(2**31)

WIDTH_A = HEADS_A * HEAD_DIM
WIDTH_B = HEADS_B * HEAD_DIM
GROUP_W = HEADS_PER_GROUP * HEAD_DIM
IDX_W = IDX_HEADS * IDX_DIM
SMALL_W = 128
MERGE_TILE = 256

V7X_VMEM_LIMIT = 56 * 1024 * 1024

_F32 = jnp.float32
_BF16 = jnp.bfloat16


def _bucket_bounds():
    n = np.arange(MAX_DISTANCE, dtype=np.int32)
    max_exact = N_BUCKETS // 2
    nf = np.maximum(n, 1).astype(np.float32)
    large = max_exact + (
        np.log(nf / np.float32(max_exact)) / np.float32(math.log(MAX_DISTANCE / max_exact))
        * np.float32(N_BUCKETS - max_exact)
    ).astype(np.int32)
    large = np.minimum(large, N_BUCKETS - 1)
    bucket = np.where(n < max_exact, n, large)
    assert np.all(np.diff(bucket) >= 0)
    bounds = [int(np.argmax(bucket >= b)) if np.any(bucket >= b) else MAX_DISTANCE for b in range(N_BUCKETS)]
    return tuple(bounds)


_BOUNDS = _bucket_bounds()


def _dot_nt(a, b):
    return lax.dot_general(a, b, (((1,), (1,)), ((), ())), preferred_element_type=_F32)


def _bias_from_distance(tab_ref, dist, head):
    val = jnp.full(dist.shape, tab_ref[head], _F32)
    for bkt in range(1, N_BUCKETS):
        val = jnp.where(dist >= _BOUNDS[bkt], tab_ref[bkt * N_HEADS + head], val)
    return val


def _rmsnorm_kernel(x_ref, g_ref, h_ref):
    x = x_ref[...]
    ms = jnp.mean(x * x, axis=-1, keepdims=True)
    h_ref[...] = (x * lax.rsqrt(ms + NORM_EPS) * g_ref[...]).astype(h_ref.dtype)


def _rmsnorm(x2d, g, tm):
    n, d = x2d.shape
    return pl.pallas_call(
        _rmsnorm_kernel,
        grid=(n // tm,),
        in_specs=[pl.BlockSpec((tm, d), lambda i: (i, 0)), pl.BlockSpec((1, d), lambda i: (0, 0))],
        out_specs=pl.BlockSpec((tm, d), lambda i: (i, 0)),
        out_shape=jax.ShapeDtypeStruct((n, d), _BF16),
        compiler_params=pltpu.CompilerParams(
            dimension_semantics=("arbitrary",), vmem_limit_bytes=V7X_VMEM_LIMIT),
        name="rmsnorm_mix",
    )(x2d, g)


def _cast_weight_tile(w_ref, wbf_ref):
    @pl.when(pl.program_id(1) == 0)
    def _():
        wbf_ref[...] = w_ref[...].astype(wbf_ref.dtype)


def _matmul_kernel(h_ref, w_ref, o_ref, wbf_ref):
    _cast_weight_tile(w_ref, wbf_ref)
    o_ref[...] = jnp.dot(h_ref[...], wbf_ref[...], preferred_element_type=_F32).astype(o_ref.dtype)


def _matmul(h, w, out_dtype, tm, tn, name, col0=0, cols=None):
    n, d = h.shape
    cols = w.shape[2] if cols is None else cols
    assert col0 % tn == 0 and cols % tn == 0
    return pl.pallas_call(
        _matmul_kernel,
        grid=(cols // tn, n // tm),
        in_specs=[pl.BlockSpec((tm, d), lambda j, i: (i, 0)),
                  pl.BlockSpec((None, d, tn), lambda j, i: (0, 0, j + col0 // tn))],
        out_specs=pl.BlockSpec((tm, tn), lambda j, i: (i, j)),
        out_shape=jax.ShapeDtypeStruct((n, cols), out_dtype),
        scratch_shapes=[pltpu.VMEM((d, tn), _BF16)],
        compiler_params=pltpu.CompilerParams(
            dimension_semantics=("arbitrary", "arbitrary"), vmem_limit_bytes=V7X_VMEM_LIMIT),
        name=name,
    )(h, w)


def _matmul_classmajor_kernel(h_ref, w_ref, o_ref, acc_ref, tmp_ref, wbf_ref, *, dil, rows):
    _cast_weight_tile(w_ref, wbf_ref)
    res = jnp.dot(h_ref[...], wbf_ref[...], preferred_element_type=_F32)
    if dil == 1:
        o_ref[0, 0] = res.astype(o_ref.dtype)
        return
    for c in range(acc_ref.shape[0]):
        acc_ref[c] = res[:, c * 128:(c + 1) * 128]
    if not _tile_class_major(dil):
        for c in range(acc_ref.shape[0]):
            for r in range(dil):
                o_ref[0, r, :, c * 128:(c + 1) * 128] = (
                    acc_ref[c, pl.ds(r, rows, stride=dil), :].astype(o_ref.dtype))
        return
    s1 = 4
    s2 = dil // s1
    step_rows = rows * s2
    for c in range(acc_ref.shape[0]):
        for r1 in range(s1):
            tmp_ref[c, r1 * step_rows:(r1 + 1) * step_rows, :] = acc_ref[c, pl.ds(r1, step_rows, stride=s1), :]
        for r1 in range(s1):
            for r2 in range(s2):
                o_ref[0, r1 + s1 * r2, :, c * 128:(c + 1) * 128] = (
                    tmp_ref[c, pl.ds(r1 * step_rows + r2, rows, stride=s2), :].astype(o_ref.dtype))


def _matmul_classmajor(h, w, col_block, n_blocks, dil, batch, seq, tm, tn, name):
    n, d = h.shape
    cols = n_blocks * tn
    tiles_per_seq = seq // tm
    rows = tm // dil
    return pl.pallas_call(
        functools.partial(_matmul_classmajor_kernel, dil=dil, rows=rows),
        grid=(n_blocks, n // tm),
        in_specs=[pl.BlockSpec((tm, d), lambda j, i: (i, 0)),
                  pl.BlockSpec((None, d, tn), lambda j, i: (0, 0, col_block(j)))],
        out_specs=pl.BlockSpec((1, dil, rows, tn),
                               lambda j, i: (i // tiles_per_seq, 0, i % tiles_per_seq, j)),
        out_shape=jax.ShapeDtypeStruct((batch, dil, seq // dil, cols), _BF16),
        scratch_shapes=[pltpu.VMEM((tn // 128, tm, 128), _F32),
                        pltpu.VMEM((tn // 128, tm if _tile_class_major(dil) else 8, 128), _F32),
                        pltpu.VMEM((d, tn), _BF16)],
        compiler_params=pltpu.CompilerParams(
            dimension_semantics=("arbitrary", "arbitrary"), vmem_limit_bytes=V7X_VMEM_LIMIT),
        name=name,
    )(h, w)


def _tile_class_major(dil):
    return dil % 8 == 0


def _mixer_a_kernel(tab_ref, a_ref, o_ref, lse_ref, bias_ref, *, group, seq):
    dil = DILATIONS[group]
    nq = seq // dil // BAND
    scale = HEAD_DIM ** -0.5

    @pl.when(pl.program_id(0) == 0)
    def _build_bias():
        row = lax.broadcasted_iota(jnp.int32, (BAND, BAND), 0)
        col = lax.broadcasted_iota(jnp.int32, (BAND, BAND), 1)
        for side in range(2):
            j = row - col + BAND * (1 - side)
            valid = jnp.where(j >= 0, j, BAND + 1) <= BAND
            for h in range(HEADS_PER_GROUP):
                val = _bias_from_distance(tab_ref, j * dil, group * HEADS_PER_GROUP + h)
                bias_ref[h, :, side * BAND:(side + 1) * BAND] = jnp.where(valid, val, NEG_INF)

    def key_rows(i):
        return pl.ds(0, BAND) if i == 0 else pl.ds((i - 1) * BAND, 2 * BAND)

    def logits(r, i, h):
        qc = slice(h * HEAD_DIM, (h + 1) * HEAD_DIM)
        kc = slice(GROUP_W + h * HEAD_DIM, GROUP_W + (h + 1) * HEAD_DIM)
        bias = bias_ref[h, :, BAND:] if i == 0 else bias_ref[h]
        return _dot_nt(a_ref[0, r, pl.ds(i * BAND, BAND), qc], a_ref[0, r, key_rows(i), kc]) * scale + bias

    def finish(r, i, h, s):
        m = jnp.max(s, axis=-1, keepdims=True)
        p = jnp.exp(s - m).astype(_BF16)
        v = a_ref[0, r, key_rows(i), 2 * GROUP_W + h * HEAD_DIM:2 * GROUP_W + (h + 1) * HEAD_DIM]
        res = jnp.dot(p, jnp.concatenate([v, jnp.ones_like(v)], axis=-1), preferred_element_type=_F32)
        den = res[:, HEAD_DIM:]
        o_val = res[:, :HEAD_DIM] / den
        lse_val = m + jnp.log(den)
        if not _tile_class_major(dil):
            out_rows = pl.ds(i * BAND, BAND) if dil == 1 else pl.ds(r + i * BAND * dil, BAND, stride=dil)
            o_ref[0, h, out_rows, :] = o_val
            lse_ref[0, h, out_rows, :] = lse_val
        else:
            per = MERGE_TILE // dil
            for c in range(BAND // per):
                rows = pl.ds((i * (BAND // per) + c) * MERGE_TILE + r * per, per)
                o_ref[0, h, rows, :] = o_val[c * per:(c + 1) * per]
                lse_ref[0, h, rows, :] = lse_val[c * per:(c + 1) * per]

    pending = None
    for r in range(dil):
        for i in range(nq):
            cur = [(r, i, h, logits(r, i, h)) for h in range(HEADS_PER_GROUP)]
            if pending is not None:
                for unit in pending:
                    finish(*unit)
            pending = cur
    for unit in pending:
        finish(*unit)


def _mixer_a_group(tab, a_g, group, batch, seq):
    dil = DILATIONS[group]
    out_spec = pl.BlockSpec((1, HEADS_PER_GROUP, seq, HEAD_DIM), lambda b: (b, 0, 0, 0))
    out_sds = jax.ShapeDtypeStruct((batch, HEADS_PER_GROUP, seq, HEAD_DIM), _F32)
    return pl.pallas_call(
        functools.partial(_mixer_a_kernel, group=group, seq=seq),
        grid=(batch,),
        in_specs=[pl.BlockSpec(memory_space=pltpu.SMEM),
                  pl.BlockSpec((1, dil, seq // dil, 3 * GROUP_W), lambda b: (b, 0, 0, 0))],
        out_specs=[out_spec, out_spec],
        out_shape=[out_sds, out_sds],
        scratch_shapes=[pltpu.VMEM((HEADS_PER_GROUP, BAND, 2 * BAND), _F32)],
        compiler_params=pltpu.CompilerParams(
            dimension_semantics=("arbitrary",), vmem_limit_bytes=V7X_VMEM_LIMIT),
        name=f"mixer_a_g{group}",
    )(tab, a_g)


def _mixer_b_kernel(tab_ref, lng_ref, lnb_ref, q_ref, k_ref, v_ref, qidx_ref, smallk_ref, smallq_ref,
                    o_ref, kidx_s, bias_s, key_s, hi_s, lo_s, selb_s, wt_s, s_s, mx_s, mrep_s, lp_s, acc_s,
                    *, tq, seq):
    b = pl.program_id(0)
    qi = pl.program_id(1)
    nkb = seq // tq
    lane_tiles = tq // 128
    scale = HEAD_DIM ** -0.5
    inv_scale = HEAD_DIM ** 0.5
    neg_scaled = NEG_INF * inv_scale
    row = lax.broadcasted_iota(jnp.int32, (tq, tq), 0)
    col = lax.broadcasted_iota(jnp.int32, (tq, tq), 1)

    @pl.when((b == 0) & (qi == 0))
    def _build_bias():
        def per_delta(dl, carry):
            dist = row - col + dl * tq
            for h in range(HEADS_B):
                bias_s[dl, h] = _bias_from_distance(tab_ref, dist, HEADS_A + h) * inv_scale
            return carry

        lax.fori_loop(0, nkb, per_delta, 0)

    @pl.when(qi == 0)
    def _layer_norm_keys():
        def chunk(c, carry):
            r = pl.multiple_of(c * tq, tq)
            kx = smallk_ref[0, pl.ds(r, tq), :][:, :IDX_DIM]
            mu = jnp.mean(kx, axis=-1, keepdims=True)
            xc = kx - mu
            var = jnp.mean(xc * xc, axis=-1, keepdims=True)
            y = xc * lax.rsqrt(var + NORM_EPS) * lng_ref[...] + lnb_ref[...]
            z = jnp.zeros_like(y)
            kidx_s[0, pl.ds(r, tq), :] = jnp.concatenate([y, z], axis=-1).astype(_BF16)
            kidx_s[1, pl.ds(r, tq), :] = jnp.concatenate([z, y], axis=-1).astype(_BF16)
            return carry

        lax.fori_loop(0, nkb, chunk, 0)

    wt_s[...] = (smallq_ref[0] * (IDX_HEADS ** -0.5 * IDX_DIM ** -0.5)).T

    def score_blk(kj, carry):
        r = pl.multiple_of(kj * tq, tq)
        acc = jnp.zeros((tq, tq), _F32)
        for h in range(IDX_HEADS):
            pair = qidx_ref[0][:, (h // 2) * 128:(h // 2 + 1) * 128]
            x = _dot_nt(kidx_s[h % 2, pl.ds(r, tq), :], pair)
            acc = acc + wt_s[IDX_DIM + h:IDX_DIM + h + 1, :] * jnp.maximum(x, 0.0)
        bits = lax.bitcast_convert_type(acc, jnp.int32)
        key = bits ^ ((bits >> 31) & 0x7FFFFFFF)
        causal = (kj * tq + row) <= (qi * tq + col)
        key = jnp.where(causal, key, INT_MIN)
        key_s[kj] = key
        hi_s[kj] = (key >> 16).astype(jnp.int16)
        lo_s[kj] = ((key & 0xFFFF) - 32768).astype(jnp.int16)
        return carry

    lax.fori_loop(0, qi + 1, score_blk, 0)

    def count_ge(cand):
        def blk(kj, cnt):
            kk = key_s[kj].reshape(tq // 8, 8, tq)
            return cnt + jnp.sum(jnp.where(kk >= cand[None], 1.0, 0.0), axis=0)

        cnt = lax.fori_loop(0, qi + 1, blk, jnp.zeros((8, tq), _F32))
        return jnp.broadcast_to(jnp.sum(cnt, axis=0, keepdims=True), (8, tq))

    one16, zero16, min16 = jnp.ones((), jnp.int16), jnp.zeros((), jnp.int16), jnp.full((), -32768, jnp.int16)

    def packed(cand):
        return jnp.broadcast_to(cand[:1], (16, tq)).astype(jnp.int16)

    def count16(src, cand, strict=False):
        c16 = packed(cand)[None]

        def blk(kj, cnt):
            kk = src[kj].reshape(tq // 16, 16, tq)
            hit = jnp.where((kk > c16) if strict else (kk >= c16), one16, zero16)
            parts = [hit[t] for t in range(tq // 16)]
            while len(parts) > 1:
                parts = [parts[t] + parts[t + 1] for t in range(0, len(parts), 2)]
            return cnt + parts[0]

        cnt = lax.fori_loop(0, qi + 1, blk, jnp.zeros((16, tq), jnp.int16))
        return jnp.broadcast_to(jnp.sum(cnt.astype(_F32), axis=0, keepdims=True), (8, tq))

    def search16(src, target, cnt_floor):
        zero = jnp.zeros((8, tq), jnp.int32)
        cnt0 = count16(src, zero)
        ok0 = cnt0 >= target
        init = (jnp.where(ok0, zero, -32768), jnp.where(ok0, cnt0, cnt_floor))

        def bit_body(bi, carry):
            prefix, cnt_prefix = carry
            cand = prefix | lax.shift_left(jnp.int32(1), 14 - bi)
            cnt = count16(src, cand)
            ok = cnt >= target
            return jnp.where(ok, cand, prefix), jnp.where(ok, cnt, cnt_prefix)

        return lax.fori_loop(0, 15, bit_body, init)

    total = jnp.zeros((8, tq), _F32) + ((qi + 1) * tq).astype(_F32)
    tau_hi, cnt_hi = search16(hi_s, float(TOPK), total)
    above_hi = count16(hi_s, tau_hi, strict=True)
    tau_hi16 = packed(tau_hi)[None]

    def mask_lo(kj, carry):
        hi = hi_s[kj].reshape(tq // 16, 16, tq)
        lo = lo_s[kj].reshape(tq // 16, 16, tq)
        lo_s[kj] = jnp.where(hi == tau_hi16, lo, min16).reshape(tq, tq)
        return carry

    lax.fori_loop(0, qi + 1, mask_lo, 0)
    tau_lo, cnt_lo = search16(lo_s, TOPK - above_hi, cnt_hi - above_hi)
    tau = tau_hi * 65536 + (tau_lo + 32768)
    cnt_tau = above_hi + cnt_lo
    thr = jnp.maximum(tau, INT_MIN + 1)
    has_ties = jnp.max(jnp.where(tau > INT_MIN, cnt_tau, 0.0)) > TOPK

    @pl.when(jnp.logical_not(has_ties))
    def _select_by_threshold():
        def write_sel(kj, carry):
            kk = key_s[kj].reshape(tq // 8, 8, tq)
            sel_t = jnp.where(kk >= thr[None], 0.0, neg_scaled).reshape(tq, tq)
            selb_s[kj] = sel_t.T
            return carry

        lax.fori_loop(0, qi + 1, write_sel, 0)

    @pl.when(has_ties)
    def _select_breaking_ties():
        need = (TOPK - count_ge(thr + 1))[:1]
        earlier = jnp.where(row > col, 1.0, 0.0).astype(_BF16)

        def write_sel(kj, seen):
            kk = key_s[kj].reshape(tq // 8, 8, tq)
            above = jnp.where(kk > thr[None], 1.0, 0.0).reshape(tq, tq)
            equal = jnp.where(kk == thr[None], 1.0, 0.0).reshape(tq, tq)
            rank = seen + jnp.dot(earlier, equal.astype(_BF16), preferred_element_type=_F32)
            take = above + equal * jnp.where(rank < need, 1.0, 0.0)
            selb_s[kj] = jnp.where(take > 0.0, 0.0, neg_scaled).T
            return seen + jnp.sum(equal, axis=0, keepdims=True)

        lax.fori_loop(0, qi + 1, write_sel, jnp.zeros((1, tq), _F32))

    head_group = s_s.shape[0]
    exp2_coeff = scale * math.log2(math.e)
    for h0 in range(0, HEADS_B, head_group):
        for hl in range(head_group):
            mx_s[hl] = jnp.full((tq, 128), -3e38, _F32)

        def logits_blk(kj, carry, h0=h0):
            r = pl.multiple_of(kj * tq, tq)
            for hl in range(head_group):
                h = h0 + hl
                cs = slice(h * HEAD_DIM, (h + 1) * HEAD_DIM)
                s = _dot_nt(q_ref[0][:, cs], k_ref[0, pl.ds(r, tq), cs]) + bias_s[qi - kj, h] + selb_s[kj]
                s_s[hl, kj] = s
                part = s[:, :128]
                for c in range(1, lane_tiles):
                    part = jnp.maximum(part, s[:, c * 128:(c + 1) * 128])
                mx_s[hl] = jnp.maximum(mx_s[hl], part)
            return carry

        lax.fori_loop(0, qi + 1, logits_blk, 0)
        for hl in range(head_group):
            m = jnp.max(mx_s[hl], axis=-1, keepdims=True)
            mrep_s[hl] = jnp.broadcast_to(m, (tq, 128))
            lp_s[hl] = jnp.zeros((tq, 128), _F32)
            acc_s[hl] = jnp.zeros((tq, HEAD_DIM), _F32)

        def pv_blk(kj, carry, h0=h0):
            r = pl.multiple_of(kj * tq, tq)
            for hl in range(head_group):
                h = h0 + hl
                cs = slice(h * HEAD_DIM, (h + 1) * HEAD_DIM)
                m = mrep_s[hl]
                p = jnp.exp2((s_s[hl, kj] - jnp.concatenate([m] * lane_tiles, axis=-1)) * exp2_coeff)
                part = p[:, :128]
                for c in range(1, lane_tiles):
                    part = part + p[:, c * 128:(c + 1) * 128]
                lp_s[hl] = lp_s[hl] + part
                acc_s[hl] = acc_s[hl] + jnp.dot(p.astype(_BF16), v_ref[0, pl.ds(r, tq), cs],
                                                preferred_element_type=_F32)
            return carry

        lax.fori_loop(0, qi + 1, pv_blk, 0)
        for hl in range(head_group):
            h = h0 + hl
            den = jnp.sum(lp_s[hl], axis=-1, keepdims=True)
            o_ref[0, :, h * HEAD_DIM:(h + 1) * HEAD_DIM] = (acc_s[hl] / den).astype(o_ref.dtype)


def _mixer_b(tab, ln_g, ln_b, big3, small3, tq):
    batch, seq, _ = big3.shape
    nkb = seq // tq
    head_group = 4
    full = lambda cb: pl.BlockSpec((1, seq, WIDTH_B), lambda b, i, cb=cb: (b, 0, cb),
                                   pipeline_mode=pl.Buffered(1))
    tile = lambda cb: pl.BlockSpec((1, tq, WIDTH_B), lambda b, i, cb=cb: (b, i, cb))
    small_cb = small3.shape[-1] // SMALL_W - 1
    return pl.pallas_call(
        functools.partial(_mixer_b_kernel, tq=tq, seq=seq),
        grid=(batch, nkb),
        in_specs=[
            pl.BlockSpec(memory_space=pltpu.SMEM),
            pl.BlockSpec((1, IDX_DIM), lambda b, i: (0, 0)),
            pl.BlockSpec((1, IDX_DIM), lambda b, i: (0, 0)),
            tile(0), full(1), full(2), tile(3),
            pl.BlockSpec((1, seq, SMALL_W), lambda b, i: (b, 0, small_cb)),
            pl.BlockSpec((1, tq, SMALL_W), lambda b, i: (b, i, small_cb)),
        ],
        out_specs=pl.BlockSpec((1, tq, WIDTH_B), lambda b, i: (b, i, 0)),
        out_shape=jax.ShapeDtypeStruct((batch, seq, WIDTH_B), _BF16),
        scratch_shapes=[
            pltpu.VMEM((2, seq, 2 * IDX_DIM), _BF16),
            pltpu.VMEM((nkb, HEADS_B, tq, tq), _F32),
            pltpu.VMEM((nkb, tq, tq), jnp.int32),
            pltpu.VMEM((nkb, tq, tq), jnp.int16),
            pltpu.VMEM((nkb, tq, tq), jnp.int16),
            pltpu.VMEM((nkb, tq, tq), _F32),
            pltpu.VMEM((SMALL_W, tq), _F32),
            pltpu.VMEM((head_group, nkb, tq, tq), _F32),
            pltpu.VMEM((head_group, tq, 128), _F32),
            pltpu.VMEM((head_group, tq, 128), _F32),
            pltpu.VMEM((head_group, tq, 128), _F32),
            pltpu.VMEM((head_group, tq, HEAD_DIM), _F32),
        ],
        compiler_params=pltpu.CompilerParams(
            dimension_semantics=("arbitrary", "arbitrary"), vmem_limit_bytes=V7X_VMEM_LIMIT),
        name="mixer_b",
    )(tab, ln_g, ln_b, big3, big3, big3, big3, small3, small3)


def _merge_kernel(x_ref, oa0_ref, oa1_ref, oa2_ref, l0_ref, l1_ref, l2_ref, ob_ref, ga_ref, gb_ref,
                  wpa_ref, wpb_ref, wout_ref, y_ref):
    def natural(ref, h, group):
        dil = DILATIONS[group]
        if not _tile_class_major(dil):
            return ref[0, h]
        per = MERGE_TILE // dil
        return jnp.concatenate([ref[0, h, pl.ds(slot, dil, stride=per), :] for slot in range(per)], axis=0)

    heads = []
    for h in range(HEADS_PER_GROUP):
        l0, l1, l2 = natural(l0_ref, h, 0), natural(l1_ref, h, 1), natural(l2_ref, h, 2)
        m = jnp.maximum(jnp.maximum(l0, l1), l2)
        e0, e1, e2 = jnp.exp(l0 - m), jnp.exp(l1 - m), jnp.exp(l2 - m)
        heads.append((e0 * natural(oa0_ref, h, 0) + e1 * natural(oa1_ref, h, 1) + e2 * natural(oa2_ref, h, 2))
                     / (e0 + e1 + e2))
    o_a = jnp.concatenate(heads, axis=-1)
    pa = jnp.dot(o_a.astype(_BF16), wpa_ref[...], preferred_element_type=_F32)
    pb = jnp.dot(ob_ref[...], wpb_ref[...], preferred_element_type=_F32)
    merged = jax.nn.sigmoid(ga_ref[...]) * pa + jax.nn.sigmoid(gb_ref[...]) * pb
    y_ref[...] = x_ref[...] + jnp.dot(merged.astype(_BF16), wout_ref[...], preferred_element_type=_F32)


def _merge(x2d, oa, lse, ob2d, fout, wpa, wpb, wout, tm):
    n, d = x2d.shape
    tiles_per_seq = oa[0].shape[2] // tm
    rows = lambda w, cb=0: pl.BlockSpec((tm, w), lambda i, cb=cb: (i, cb))
    heads = pl.BlockSpec((1, HEADS_PER_GROUP, tm, HEAD_DIM),
                         lambda i: (i // tiles_per_seq, 0, i % tiles_per_seq, 0))
    const = lambda shape: pl.BlockSpec(shape, lambda i: (0, 0), pipeline_mode=pl.Buffered(1))
    return pl.pallas_call(
        _merge_kernel,
        grid=(n // tm,),
        in_specs=[rows(d), heads, heads, heads, heads, heads, heads, rows(WIDTH_B), rows(d, 0), rows(d, 1),
                  const(wpa.shape), const(wpb.shape), const(wout.shape)],
        out_specs=rows(d),
        out_shape=jax.ShapeDtypeStruct((n, d), _F32),
        compiler_params=pltpu.CompilerParams(
            dimension_semantics=("arbitrary",), vmem_limit_bytes=V7X_VMEM_LIMIT),
        name="merge_out_proj",
    )(x2d, oa[0], oa[1], oa[2], lse[0], lse[1], lse[2], ob2d, fout, fout, wpa, wpb, wout)


def _mlp_kernel(x_ref, g_ref, gf_ref, wu_ref, wd_ref, y_ref, h_ref):
    j = pl.program_id(1)

    @pl.when(j == 0)
    def _():
        x = x_ref[...]
        ms = jnp.mean(x * x, axis=-1, keepdims=True)
        h_ref[...] = (x * lax.rsqrt(ms + NORM_EPS) * g_ref[...]).astype(_BF16)
        y_ref[...] = x

    u = jnp.maximum(jnp.dot(h_ref[...], wu_ref[...], preferred_element_type=_F32), 0.0)
    y_ref[...] += jnp.dot((u * u).astype(_BF16), wd_ref[...], preferred_element_type=_F32)

    @pl.when(j == pl.num_programs(1) - 1)
    def _():
        y = y_ref[...]
        ms = jnp.mean(y * y, axis=-1, keepdims=True)
        y_ref[...] = y * lax.rsqrt(ms + NORM_EPS) * gf_ref[...]


def _mlp(x2d, g, gf, wu, wd, tm, tf):
    n, d = x2d.shape
    dff = wu.shape[1]
    return pl.pallas_call(
        _mlp_kernel,
        grid=(n // tm, dff // tf),
        in_specs=[
            pl.BlockSpec((tm, d), lambda i, j: (i, 0), pipeline_mode=pl.Buffered(1)),
            pl.BlockSpec((1, d), lambda i, j: (0, 0)),
            pl.BlockSpec((1, d), lambda i, j: (0, 0)),
            pl.BlockSpec((d, tf), lambda i, j: (0, j)),
            pl.BlockSpec((tf, d), lambda i, j: (j, 0)),
        ],
        out_specs=pl.BlockSpec((tm, d), lambda i, j: (i, 0)),
        out_shape=jax.ShapeDtypeStruct((n, d), _F32),
        scratch_shapes=[pltpu.VMEM((tm, d), _BF16)],
        compiler_params=pltpu.CompilerParams(
            dimension_semantics=("arbitrary", "arbitrary"), vmem_limit_bytes=V7X_VMEM_LIMIT),
        name="mlp_final_norm",
    )(x2d, g, gf, wu, wd)


def _largest_tile(n, limit, step):
    t = (min(n, limit) // step) * step
    while n % t:
        t -= step
    return t


def kernel(x, norm_mix_g, w_in, idx_k_norm_g, idx_k_norm_b, rel_bias_table, w_proj_a, w_proj_b, w_out,
           norm_mlp_g, w_mlp_up, w_mlp_down, norm_final_g):
    batch, seq, d = x.shape
    assert seq == MAX_DISTANCE and seq % (DILATIONS[-1] * BAND) == 0
    assert w_in.shape[0] == 1, "single layer"
    n = batch * seq
    x2d = x.reshape(n, d)
    tab = rel_bias_table.reshape(-1)

    b0 = 3 * WIDTH_A
    k0 = b0 + 3 * WIDTH_B + IDX_W
    ga0 = k0 + IDX_DIM + IDX_HEADS
    w_gates = jnp.concatenate(
        [w_in[:, :, ga0:ga0 + 2 * d], w_in[:, :, k0:ga0],
         jnp.zeros((1, d, SMALL_W - IDX_DIM - IDX_HEADS), w_in.dtype)], axis=2).astype(_BF16)

    h = _rmsnorm(x2d, norm_mix_g[0].reshape(1, d), _largest_tile(n, 1024, 8))
    big_b = _matmul(h, w_in, _BF16, seq, GROUP_W, "proj_b_idx", col0=b0, cols=k0 - b0)
    fout = _matmul(h, w_gates, _F32, _largest_tile(seq, 1024, 256),
                   _largest_tile(w_gates.shape[2], 1408, 128), "proj_gates")

    oa, lse = [], []
    for group in range(N_GROUPS):
        a_g = _matmul_classmajor(h, w_in, lambda j, group=group: j * N_GROUPS + group, 3, DILATIONS[group],
                                 batch, seq, seq, GROUP_W, f"proj_a_g{group}")
        o_g, lse_g = _mixer_a_group(tab, a_g, group, batch, seq)
        oa.append(o_g)
        lse.append(lse_g)

    ob = _mixer_b(tab, idx_k_norm_g[0].reshape(1, IDX_DIM), idx_k_norm_b[0].reshape(1, IDX_DIM),
                  big_b.reshape(batch, seq, -1), fout.reshape(batch, seq, -1), tq=256)

    x1 = _merge(x2d, oa, lse, ob.reshape(n, WIDTH_B), fout,
                w_proj_a[0].astype(_BF16), w_proj_b[0].astype(_BF16), w_out[0].astype(_BF16),
                tm=MERGE_TILE)
    y = _mlp(x1, norm_mlp_g[0].reshape(1, d), norm_final_g.reshape(1, d),
             w_mlp_up[0].astype(_BF16), w_mlp_down[0].astype(_BF16),
             tm=_largest_tile(n, 1024, 128), tf=_largest_tile(w_mlp_up.shape[-1], 1024, 128))
    return y.reshape(batch, seq, d)
```

```python
import functools
import math

import jax
import jax.numpy as jnp
import numpy as np
from jax import lax
from jax.experimental import pallas as pl
from jax.experimental.pallas import tpu as pltpu

HEAD_DIM = 128
DILATIONS = (1, 4, 16)
N_GROUPS = 3
HEADS_PER_GROUP = 4
HEADS_A = N_GROUPS * HEADS_PER_GROUP
BAND = 128
HEADS_B = 8
N_HEADS = HEADS_A + HEADS_B
IDX_HEADS = 16
IDX_DIM = 64
TOPK = 256
N_BUCKETS = 32
MAX_DISTANCE = 2048
NORM_EPS = 1e-6
NEG_INF = -1e30
INT_MIN = -(2**31)

WIDTH_A = HEADS_A * HEAD_DIM
WIDTH_B = HEADS_B * HEAD_DIM
GROUP_W = HEADS_PER_GROUP * HEAD_DIM
IDX_W = IDX_HEADS * IDX_DIM
SMALL_W = 128
MERGE_TILE = 256

V7X_VMEM_LIMIT = 56 * 1024 * 1024

_F32 = jnp.float32
_BF16 = jnp.bfloat16


def _bucket_bounds():
    n = np.arange(MAX_DISTANCE, dtype=np.int32)
    max_exact = N_BUCKETS // 2
    nf = np.maximum(n, 1).astype(np.float32)
    large = max_exact + (
        np.log(nf / np.float32(max_exact)) / np.float32(math.log(MAX_DISTANCE / max_exact))
        * np.float32(N_BUCKETS - max_exact)
    ).astype(np.int32)
    large = np.minimum(large, N_BUCKETS - 1)
    bucket = np.where(n < max_exact, n, large)
    assert np.all(np.diff(bucket) >= 0)
    bounds = [int(np.argmax(bucket >= b)) if np.any(bucket >= b) else MAX_DISTANCE for b in range(N_BUCKETS)]
    return tuple(bounds)


_BOUNDS = _bucket_bounds()


def _dot_nt(a, b):
    return lax.dot_general(a, b, (((1,), (1,)), ((), ())), preferred_element_type=_F32)


def _bias_from_distance(tab_ref, dist, head):
    val = jnp.full(dist.shape, tab_ref[head], _F32)
    for bkt in range(1, N_BUCKETS):
        val = jnp.where(dist >= _BOUNDS[bkt], tab_ref[bkt * N_HEADS + head], val)
    return val


def _rmsnorm_kernel(x_ref, g_ref, h_ref):
    x = x_ref[...]
    ms = jnp.mean(x * x, axis=-1, keepdims=True)
    h_ref[...] = (x * lax.rsqrt(ms + NORM_EPS) * g_ref[...]).astype(h_ref.dtype)


def _rmsnorm(x2d, g, tm):
    n, d = x2d.shape
    return pl.pallas_call(
        _rmsnorm_kernel,
        grid=(n // tm,),
        in_specs=[pl.BlockSpec((tm, d), lambda i: (i, 0)), pl.BlockSpec((1, d), lambda i: (0, 0))],
        out_specs=pl.BlockSpec((tm, d), lambda i: (i, 0)),
        out_shape=jax.ShapeDtypeStruct((n, d), _BF16),
        compiler_params=pltpu.CompilerParams(
            dimension_semantics=("arbitrary",), vmem_limit_bytes=V7X_VMEM_LIMIT),
        name="rmsnorm_mix",
    )(x2d, g)


def _cast_weight_tile(w_ref, wbf_ref):
    @pl.when(pl.program_id(1) == 0)
    def _():
        wbf_ref[...] = w_ref[...].astype(wbf_ref.dtype)


def _matmul_kernel(h_ref, w_ref, o_ref, wbf_ref):
    _cast_weight_tile(w_ref, wbf_ref)
    o_ref[...] = _dot_nt(h_ref[...], wbf_ref[...]).astype(o_ref.dtype)


def _weight_rows_spec(tn, d, row_start):
    return pl.BlockSpec((pl.Element(tn), pl.Element(d)), lambda j, i: (pl.multiple_of(row_start(j), 8), 0))


def _matmul(h, wt, out_dtype, tm, tn, name, row0, cols):
    n, d = h.shape
    assert cols % tn == 0
    return pl.pallas_call(
        _matmul_kernel,
        grid=(cols // tn, n // tm),
        in_specs=[pl.BlockSpec((tm, d), lambda j, i: (i, 0)),
                  _weight_rows_spec(tn, d, lambda j: row0 + j * tn)],
        out_specs=pl.BlockSpec((tm, tn), lambda j, i: (i, j)),
        out_shape=jax.ShapeDtypeStruct((n, cols), out_dtype),
        scratch_shapes=[pltpu.VMEM((tn, d), _BF16)],
        compiler_params=pltpu.CompilerParams(
            dimension_semantics=("arbitrary", "arbitrary"), vmem_limit_bytes=V7X_VMEM_LIMIT),
        name=name,
    )(h, wt)


def _matmul_classmajor_kernel(h_ref, w_ref, o_ref, acc_ref, tmp_ref, wbf_ref, *, dil, rows):
    _cast_weight_tile(w_ref, wbf_ref)
    res = _dot_nt(h_ref[...], wbf_ref[...])
    if dil == 1:
        o_ref[0, 0] = res.astype(o_ref.dtype)
        return
    for c in range(acc_ref.shape[0]):
        acc_ref[c] = res[:, c * 128:(c + 1) * 128]
    if not _tile_class_major(dil):
        for c in range(acc_ref.shape[0]):
            for r in range(dil):
                o_ref[0, r, :, c * 128:(c + 1) * 128] = (
                    acc_ref[c, pl.ds(r, rows, stride=dil), :].astype(o_ref.dtype))
        return
    s1 = 4
    s2 = dil // s1
    step_rows = rows * s2
    for c in range(acc_ref.shape[0]):
        for r1 in range(s1):
            tmp_ref[c, r1 * step_rows:(r1 + 1) * step_rows, :] = acc_ref[c, pl.ds(r1, step_rows, stride=s1), :]
        for r1 in range(s1):
            for r2 in range(s2):
                o_ref[0, r1 + s1 * r2, :, c * 128:(c + 1) * 128] = (
                    tmp_ref[c, pl.ds(r1 * step_rows + r2, rows, stride=s2), :].astype(o_ref.dtype))


def _matmul_classmajor(h, wt, row_start, n_blocks, dil, batch, seq, tm, tn, name):
    n, d = h.shape
    cols = n_blocks * tn
    tiles_per_seq = seq // tm
    rows = tm // dil
    return pl.pallas_call(
        functools.partial(_matmul_classmajor_kernel, dil=dil, rows=rows),
        grid=(n_blocks, n // tm),
        in_specs=[pl.BlockSpec((tm, d), lambda j, i: (i, 0)), _weight_rows_spec(tn, d, row_start)],
        out_specs=pl.BlockSpec((1, dil, rows, tn),
                               lambda j, i: (i // tiles_per_seq, 0, i % tiles_per_seq, j)),
        out_shape=jax.ShapeDtypeStruct((batch, dil, seq // dil, cols), _BF16),
        scratch_shapes=[pltpu.VMEM((tn // 128, tm, 128), _F32),
                        pltpu.VMEM((tn // 128, tm if _tile_class_major(dil) else 8, 128), _F32),
                        pltpu.VMEM((tn, d), _BF16)],
        compiler_params=pltpu.CompilerParams(
            dimension_semantics=("arbitrary", "arbitrary"), vmem_limit_bytes=V7X_VMEM_LIMIT),
        name=name,
    )(h, wt)


def _tile_class_major(dil):
    return dil % 8 == 0


def _mixer_a_kernel(tab_ref, a_ref, o_ref, lse_ref, bias_ref, *, group, seq):
    dil = DILATIONS[group]
    nq = seq // dil // BAND
    scale = HEAD_DIM ** -0.5

    @pl.when(pl.program_id(0) == 0)
    def _build_bias():
        row = lax.broadcasted_iota(jnp.int32, (BAND, BAND), 0)
        col = lax.broadcasted_iota(jnp.int32, (BAND, BAND), 1)
        for side in range(2):
            j = row - col + BAND * (1 - side)
            valid = jnp.where(j >= 0, j, BAND + 1) <= BAND
            for h in range(HEADS_PER_GROUP):
                val = _bias_from_distance(tab_ref, j * dil, group * HEADS_PER_GROUP + h)
                bias_ref[h, :, side * BAND:(side + 1) * BAND] = jnp.where(valid, val, NEG_INF)

    def key_rows(i):
        return pl.ds(0, BAND) if i == 0 else pl.ds((i - 1) * BAND, 2 * BAND)

    def logits(r, i, h):
        qc = slice(h * HEAD_DIM, (h + 1) * HEAD_DIM)
        kc = slice(GROUP_W + h * HEAD_DIM, GROUP_W + (h + 1) * HEAD_DIM)
        bias = bias_ref[h, :, BAND:] if i == 0 else bias_ref[h]
        return _dot_nt(a_ref[0, r, pl.ds(i * BAND, BAND), qc], a_ref[0, r, key_rows(i), kc]) * scale + bias

    def finish(r, i, h, s):
        m = jnp.max(s, axis=-1, keepdims=True)
        p = jnp.exp(s - m).astype(_BF16)
        v = a_ref[0, r, key_rows(i), 2 * GROUP_W + h * HEAD_DIM:2 * GROUP_W + (h + 1) * HEAD_DIM]
        res = jnp.dot(p, jnp.concatenate([v, jnp.ones_like(v)], axis=-1), preferred_element_type=_F32)
        den = res[:, HEAD_DIM:]
        o_val = res[:, :HEAD_DIM] / den
        lse_val = m + jnp.log(den)
        if not _tile_class_major(dil):
            out_rows = pl.ds(i * BAND, BAND) if dil == 1 else pl.ds(r + i * BAND * dil, BAND, stride=dil)
            o_ref[0, h, out_rows, :] = o_val
            lse_ref[0, h, out_rows, :] = lse_val
        else:
            per = MERGE_TILE // dil
            for c in range(BAND // per):
                rows = pl.ds((i * (BAND // per) + c) * MERGE_TILE + r * per, per)
                o_ref[0, h, rows, :] = o_val[c * per:(c + 1) * per]
                lse_ref[0, h, rows, :] = lse_val[c * per:(c + 1) * per]

    pending = None
    for r in range(dil):
        for i in range(nq):
            cur = [(r, i, h, logits(r, i, h)) for h in range(HEADS_PER_GROUP)]
            if pending is not None:
                for unit in pending:
                    finish(*unit)
            pending = cur
    for unit in pending:
        finish(*unit)


def _mixer_a_group(tab, a_g, group, batch, seq):
    dil = DILATIONS[group]
    out_spec = pl.BlockSpec((1, HEADS_PER_GROUP, seq, HEAD_DIM), lambda b: (b, 0, 0, 0))
    out_sds = jax.ShapeDtypeStruct((batch, HEADS_PER_GROUP, seq, HEAD_DIM), _F32)
    return pl.pallas_call(
        functools.partial(_mixer_a_kernel, group=group, seq=seq),
        grid=(batch,),
        in_specs=[pl.BlockSpec(memory_space=pltpu.SMEM),
                  pl.BlockSpec((1, dil, seq // dil, 3 * GROUP_W), lambda b: (b, 0, 0, 0))],
        out_specs=[out_spec, out_spec],
        out_shape=[out_sds, out_sds],
        scratch_shapes=[pltpu.VMEM((HEADS_PER_GROUP, BAND, 2 * BAND), _F32)],
        compiler_params=pltpu.CompilerParams(
            dimension_semantics=("arbitrary",), vmem_limit_bytes=V7X_VMEM_LIMIT),
        name=f"mixer_a_g{group}",
    )(tab, a_g)


def _mixer_b_kernel(tab_ref, lng_ref, lnb_ref, q_ref, k_ref, v_ref, qidx_ref, smallk_ref, smallq_ref,
                    o_ref, kidx_s, bias_s, key_s, hi_s, lo_s, selb_s, wt_s, s_s, mx_s, mrep_s, lp_s, acc_s,
                    *, tq, seq):
    b = pl.program_id(0)
    qi = pl.program_id(1)
    nkb = seq // tq
    lane_tiles = tq // 128
    scale = HEAD_DIM ** -0.5
    inv_scale = HEAD_DIM ** 0.5
    neg_scaled = NEG_INF * inv_scale
    row = lax.broadcasted_iota(jnp.int32, (tq, tq), 0)
    col = lax.broadcasted_iota(jnp.int32, (tq, tq), 1)

    @pl.when((b == 0) & (qi == 0))
    def _build_bias():
        def per_delta(dl, carry):
            dist = row - col + dl * tq
            for h in range(HEADS_B):
                bias_s[dl, h] = _bias_from_distance(tab_ref, dist, HEADS_A + h) * inv_scale
            return carry

        lax.fori_loop(0, nkb, per_delta, 0)

    @pl.when(qi == 0)
    def _layer_norm_keys():
        def chunk(c, carry):
            r = pl.multiple_of(c * tq, tq)
            kx = smallk_ref[0, pl.ds(r, tq), :][:, :IDX_DIM]
            mu = jnp.mean(kx, axis=-1, keepdims=True)
            xc = kx - mu
            var = jnp.mean(xc * xc, axis=-1, keepdims=True)
            y = xc * lax.rsqrt(var + NORM_EPS) * lng_ref[...] + lnb_ref[...]
            z = jnp.zeros_like(y)
            kidx_s[0, pl.ds(r, tq), :] = jnp.concatenate([y, z], axis=-1).astype(_BF16)
            kidx_s[1, pl.ds(r, tq), :] = jnp.concatenate([z, y], axis=-1).astype(_BF16)
            return carry

        lax.fori_loop(0, nkb, chunk, 0)

    wt_s[...] = (smallq_ref[0] * (IDX_HEADS ** -0.5 * IDX_DIM ** -0.5)).T

    def score_blk(kj, carry):
        r = pl.multiple_of(kj * tq, tq)
        acc = jnp.zeros((tq, tq), _F32)
        for h in range(IDX_HEADS):
            pair = qidx_ref[0][:, (h // 2) * 128:(h // 2 + 1) * 128]
            x = _dot_nt(kidx_s[h % 2, pl.ds(r, tq), :], pair)
            acc = acc + wt_s[IDX_DIM + h:IDX_DIM + h + 1, :] * jnp.maximum(x, 0.0)
        bits = lax.bitcast_convert_type(acc, jnp.int32)
        key = bits ^ ((bits >> 31) & 0x7FFFFFFF)
        causal = (kj * tq + row) <= (qi * tq + col)
        key = jnp.where(causal, key, INT_MIN)
        key_s[kj] = key
        hi_s[kj] = (key >> 16).astype(jnp.int16)
        lo_s[kj] = ((key & 0xFFFF) - 32768).astype(jnp.int16)
        return carry

    lax.fori_loop(0, qi + 1, score_blk, 0)

    def count_ge(cand):
        def blk(kj, cnt):
            kk = key_s[kj].reshape(tq // 8, 8, tq)
            return cnt + jnp.sum(jnp.where(kk >= cand[None], 1.0, 0.0), axis=0)

        cnt = lax.fori_loop(0, qi + 1, blk, jnp.zeros((8, tq), _F32))
        return jnp.broadcast_to(jnp.sum(cnt, axis=0, keepdims=True), (8, tq))

    one16, zero16, min16 = jnp.ones((), jnp.int16), jnp.zeros((), jnp.int16), jnp.full((), -32768, jnp.int16)

    def packed(cand):
        return jnp.broadcast_to(cand[:1], (16, tq)).astype(jnp.int16)

    def count16(src, cand, strict=False):
        c16 = packed(cand)[None]

        def blk(kj, cnt):
            kk = src[kj].reshape(tq // 16, 16, tq)
            hit = jnp.where((kk > c16) if strict else (kk >= c16), one16, zero16)
            parts = [hit[t] for t in range(tq // 16)]
            while len(parts) > 1:
                parts = [parts[t] + parts[t + 1] for t in range(0, len(parts), 2)]
            return cnt + parts[0]

        cnt = lax.fori_loop(0, qi + 1, blk, jnp.zeros((16, tq), jnp.int16))
        return jnp.broadcast_to(jnp.sum(cnt.astype(_F32), axis=0, keepdims=True), (8, tq))

    def search16(src, target, cnt_floor):
        zero = jnp.zeros((8, tq), jnp.int32)
        cnt0 = count16(src, zero)
        ok0 = cnt0 >= target
        init = (jnp.where(ok0, zero, -32768), jnp.where(ok0, cnt0, cnt_floor))

        def bit_body(bi, carry):
            prefix, cnt_prefix = carry
            cand = prefix | lax.shift_left(jnp.int32(1), 14 - bi)
            cnt = count16(src, cand)
            ok = cnt >= target
            return jnp.where(ok, cand, prefix), jnp.where(ok, cnt, cnt_prefix)

        return lax.fori_loop(0, 15, bit_body, init)

    total = jnp.zeros((8, tq), _F32) + ((qi + 1) * tq).astype(_F32)
    tau_hi, cnt_hi = search16(hi_s, float(TOPK), total)
    above_hi = count16(hi_s, tau_hi, strict=True)
    tau_hi16 = packed(tau_hi)[None]

    def mask_lo(kj, carry):
        hi = hi_s[kj].reshape(tq // 16, 16, tq)
        lo = lo_s[kj].reshape(tq // 16, 16, tq)
        lo_s[kj] = jnp.where(hi == tau_hi16, lo, min16).reshape(tq, tq)
        return carry

    lax.fori_loop(0, qi + 1, mask_lo, 0)
    tau_lo, cnt_lo = search16(lo_s, TOPK - above_hi, cnt_hi - above_hi)
    tau = tau_hi * 65536 + (tau_lo + 32768)
    cnt_tau = above_hi + cnt_lo
    thr = jnp.maximum(tau, INT_MIN + 1)
    has_ties = jnp.max(jnp.where(tau > INT_MIN, cnt_tau, 0.0)) > TOPK

    @pl.when(jnp.logical_not(has_ties))
    def _select_by_threshold():
        def write_sel(kj, carry):
            kk = key_s[kj].reshape(tq // 8, 8, tq)
            sel_t = jnp.where(kk >= thr[None], 0.0, neg_scaled).reshape(tq, tq)
            selb_s[kj] = sel_t.T
            return carry

        lax.fori_loop(0, qi + 1, write_sel, 0)

    @pl.when(has_ties)
    def _select_breaking_ties():
        need = (TOPK - count_ge(thr + 1))[:1]
        earlier = jnp.where(row > col, 1.0, 0.0).astype(_BF16)

        def write_sel(kj, seen):
            kk = key_s[kj].reshape(tq // 8, 8, tq)
            above = jnp.where(kk > thr[None], 1.0, 0.0).reshape(tq, tq)
            equal = jnp.where(kk == thr[None], 1.0, 0.0).reshape(tq, tq)
            rank = seen + jnp.dot(earlier, equal.astype(_BF16), preferred_element_type=_F32)
            take = above + equal * jnp.where(rank < need, 1.0, 0.0)
            selb_s[kj] = jnp.where(take > 0.0, 0.0, neg_scaled).T
            return seen + jnp.sum(equal, axis=0, keepdims=True)

        lax.fori_loop(0, qi + 1, write_sel, jnp.zeros((1, tq), _F32))

    head_group = s_s.shape[0]
    exp2_coeff = scale * math.log2(math.e)
    for h0 in range(0, HEADS_B, head_group):
        for hl in range(head_group):
            mx_s[hl] = jnp.full((tq, 128), -3e38, _F32)

        def logits_blk(kj, carry, h0=h0):
            r = pl.multiple_of(kj * tq, tq)
            for hl in range(head_group):
                h = h0 + hl
                cs = slice(h * HEAD_DIM, (h + 1) * HEAD_DIM)
                s = _dot_nt(q_ref[0][:, cs], k_ref[0, pl.ds(r, tq), cs]) + bias_s[qi - kj, h] + selb_s[kj]
                s_s[hl, kj] = s
                part = s[:, :128]
                for c in range(1, lane_tiles):
                    part = jnp.maximum(part, s[:, c * 128:(c + 1) * 128])
                mx_s[hl] = jnp.maximum(mx_s[hl], part)
            return carry

        lax.fori_loop(0, qi + 1, logits_blk, 0)
        for hl in range(head_group):
            m = jnp.max(mx_s[hl], axis=-1, keepdims=True)
            mrep_s[hl] = jnp.broadcast_to(m, (tq, 128))
            lp_s[hl] = jnp.zeros((tq, 128), _F32)
            acc_s[hl] = jnp.zeros((tq, HEAD_DIM), _F32)

        def pv_blk(kj, carry, h0=h0):
            r = pl.multiple_of(kj * tq, tq)
            for hl in range(head_group):
                h = h0 + hl
                cs = slice(h * HEAD_DIM, (h + 1) * HEAD_DIM)
                m = mrep_s[hl]
                p = jnp.exp2((s_s[hl, kj] - jnp.concatenate([m] * lane_tiles, axis=-1)) * exp2_coeff)
                part = p[:, :128]
                for c in range(1, lane_tiles):
                    part = part + p[:, c * 128:(c + 1) * 128]
                lp_s[hl] = lp_s[hl] + part
                acc_s[hl] = acc_s[hl] + jnp.dot(p.astype(_BF16), v_ref[0, pl.ds(r, tq), cs],
                                                preferred_element_type=_F32)
            return carry

        lax.fori_loop(0, qi + 1, pv_blk, 0)
        for hl in range(head_group):
            h = h0 + hl
            den = jnp.sum(lp_s[hl], axis=-1, keepdims=True)
            o_ref[0, :, h * HEAD_DIM:(h + 1) * HEAD_DIM] = (acc_s[hl] / den).astype(o_ref.dtype)


def _mixer_b(tab, ln_g, ln_b, big3, small3, tq):
    batch, seq, _ = big3.shape
    nkb = seq // tq
    head_group = 4
    full = lambda cb: pl.BlockSpec((1, seq, WIDTH_B), lambda b, i, cb=cb: (b, 0, cb),
                                   pipeline_mode=pl.Buffered(1))
    tile = lambda cb: pl.BlockSpec((1, tq, WIDTH_B), lambda b, i, cb=cb: (b, i, cb))
    small_cb = small3.shape[-1] // SMALL_W - 1
    return pl.pallas_call(
        functools.partial(_mixer_b_kernel, tq=tq, seq=seq),
        grid=(batch, nkb),
        in_specs=[
            pl.BlockSpec(memory_space=pltpu.SMEM),
            pl.BlockSpec((1, IDX_DIM), lambda b, i: (0, 0)),
            pl.BlockSpec((1, IDX_DIM), lambda b, i: (0, 0)),
            tile(0), full(1), full(2), tile(3),
            pl.BlockSpec((1, seq, SMALL_W), lambda b, i: (b, 0, small_cb)),
            pl.BlockSpec((1, tq, SMALL_W), lambda b, i: (b, i, small_cb)),
        ],
        out_specs=pl.BlockSpec((1, tq, WIDTH_B), lambda b, i: (b, i, 0)),
        out_shape=jax.ShapeDtypeStruct((batch, seq, WIDTH_B), _BF16),
        scratch_shapes=[
            pltpu.VMEM((2, seq, 2 * IDX_DIM), _BF16),
            pltpu.VMEM((nkb, HEADS_B, tq, tq), _F32),
            pltpu.VMEM((nkb, tq, tq), jnp.int32),
            pltpu.VMEM((nkb, tq, tq), jnp.int16),
            pltpu.VMEM((nkb, tq, tq), jnp.int16),
            pltpu.VMEM((nkb, tq, tq), _F32),
            pltpu.VMEM((SMALL_W, tq), _F32),
            pltpu.VMEM((head_group, nkb, tq, tq), _F32),
            pltpu.VMEM((head_group, tq, 128), _F32),
            pltpu.VMEM((head_group, tq, 128), _F32),
            pltpu.VMEM((head_group, tq, 128), _F32),
            pltpu.VMEM((head_group, tq, HEAD_DIM), _F32),
        ],
        compiler_params=pltpu.CompilerParams(
            dimension_semantics=("arbitrary", "arbitrary"), vmem_limit_bytes=V7X_VMEM_LIMIT),
        name="mixer_b",
    )(tab, ln_g, ln_b, big3, big3, big3, big3, small3, small3)


def _merge_kernel(x_ref, oa0_ref, oa1_ref, oa2_ref, l0_ref, l1_ref, l2_ref, ob_ref, ga_ref, gb_ref,
                  wpa_ref, wpb_ref, wout_ref, y_ref):
    def natural(ref, h, group):
        dil = DILATIONS[group]
        if not _tile_class_major(dil):
            return ref[0, h]
        per = MERGE_TILE // dil
        return jnp.concatenate([ref[0, h, pl.ds(slot, dil, stride=per), :] for slot in range(per)], axis=0)

    heads = []
    for h in range(HEADS_PER_GROUP):
        l0, l1, l2 = natural(l0_ref, h, 0), natural(l1_ref, h, 1), natural(l2_ref, h, 2)
        m = jnp.maximum(jnp.maximum(l0, l1), l2)
        e0, e1, e2 = jnp.exp(l0 - m), jnp.exp(l1 - m), jnp.exp(l2 - m)
        heads.append((e0 * natural(oa0_ref, h, 0) + e1 * natural(oa1_ref, h, 1) + e2 * natural(oa2_ref, h, 2))
                     / (e0 + e1 + e2))
    o_a = jnp.concatenate(heads, axis=-1)
    pa = jnp.dot(o_a.astype(_BF16), wpa_ref[...], preferred_element_type=_F32)
    pb = jnp.dot(ob_ref[...], wpb_ref[...], preferred_element_type=_F32)
    merged = jax.nn.sigmoid(ga_ref[...]) * pa + jax.nn.sigmoid(gb_ref[...]) * pb
    y_ref[...] = x_ref[...] + jnp.dot(merged.astype(_BF16), wout_ref[...], preferred_element_type=_F32)


def _merge(x2d, oa, lse, ob2d, fout, wpa, wpb, wout, tm):
    n, d = x2d.shape
    tiles_per_seq = oa[0].shape[2] // tm
    rows = lambda w, cb=0: pl.BlockSpec((tm, w), lambda i, cb=cb: (i, cb))
    heads = pl.BlockSpec((1, HEADS_PER_GROUP, tm, HEAD_DIM),
                         lambda i: (i // tiles_per_seq, 0, i % tiles_per_seq, 0))
    const = lambda shape: pl.BlockSpec(shape, lambda i: (0, 0), pipeline_mode=pl.Buffered(1))
    return pl.pallas_call(
        _merge_kernel,
        grid=(n // tm,),
        in_specs=[rows(d), heads, heads, heads, heads, heads, heads, rows(WIDTH_B), rows(d, 0), rows(d, 1),
                  const(wpa.shape), const(wpb.shape), const(wout.shape)],
        out_specs=rows(d),
        out_shape=jax.ShapeDtypeStruct((n, d), _F32),
        compiler_params=pltpu.CompilerParams(
            dimension_semantics=("arbitrary",), vmem_limit_bytes=V7X_VMEM_LIMIT),
        name="merge_out_proj",
    )(x2d, oa[0], oa[1], oa[2], lse[0], lse[1], lse[2], ob2d, fout, fout, wpa, wpb, wout)


def _mlp_kernel(x_ref, g_ref, gf_ref, wu_ref, wd_ref, y_ref, h_ref):
    j = pl.program_id(1)

    @pl.when(j == 0)
    def _():
        x = x_ref[...]
        ms = jnp.mean(x * x, axis=-1, keepdims=True)
        h_ref[...] = (x * lax.rsqrt(ms + NORM_EPS) * g_ref[...]).astype(_BF16)
        y_ref[...] = x

    u = jnp.maximum(jnp.dot(h_ref[...], wu_ref[...], preferred_element_type=_F32), 0.0)
    y_ref[...] += jnp.dot((u * u).astype(_BF16), wd_ref[...], preferred_element_type=_F32)

    @pl.when(j == pl.num_programs(1) - 1)
    def _():
        y = y_ref[...]
        ms = jnp.mean(y * y, axis=-1, keepdims=True)
        y_ref[...] = y * lax.rsqrt(ms + NORM_EPS) * gf_ref[...]


def _mlp(x2d, g, gf, wu, wd, tm, tf):
    n, d = x2d.shape
    dff = wu.shape[1]
    return pl.pallas_call(
        _mlp_kernel,
        grid=(n // tm, dff // tf),
        in_specs=[
            pl.BlockSpec((tm, d), lambda i, j: (i, 0)),
            pl.BlockSpec((1, d), lambda i, j: (0, 0)),
            pl.BlockSpec((1, d), lambda i, j: (0, 0)),
            pl.BlockSpec((d, tf), lambda i, j: (0, j)),
            pl.BlockSpec((tf, d), lambda i, j: (j, 0)),
        ],
        out_specs=pl.BlockSpec((tm, d), lambda i, j: (i, 0)),
        out_shape=jax.ShapeDtypeStruct((n, d), _F32),
        scratch_shapes=[pltpu.VMEM((tm, d), _BF16)],
        compiler_params=pltpu.CompilerParams(
            dimension_semantics=("arbitrary", "arbitrary"), vmem_limit_bytes=V7X_VMEM_LIMIT),
        name="mlp_final_norm",
    )(x2d, g, gf, wu, wd)


def _largest_tile(n, limit, step):
    t = (min(n, limit) // step) * step
    while n % t:
        t -= step
    return t


def kernel(x, norm_mix_g, w_in, idx_k_norm_g, idx_k_norm_b, rel_bias_table, w_proj_a, w_proj_b, w_out,
           norm_mlp_g, w_mlp_up, w_mlp_down, norm_final_g):
    batch, seq, d = x.shape
    assert seq == MAX_DISTANCE and seq % (DILATIONS[-1] * BAND) == 0
    assert w_in.shape[0] == 1, "single layer"
    n = batch * seq
    x2d = x.reshape(n, d)
    tab = rel_bias_table.reshape(-1)

    b0 = 3 * WIDTH_A
    k0 = b0 + 3 * WIDTH_B + IDX_W
    ga0 = k0 + IDX_DIM + IDX_HEADS
    wt = w_in[0].T

    h = _rmsnorm(x2d, norm_mix_g[0].reshape(1, d), _largest_tile(n, 1024, 8))
    big_b = _matmul(h, wt, _BF16, seq, GROUP_W, "proj_b_idx", row0=b0, cols=k0 - b0)
    gates = _matmul(h, wt, _F32, _largest_tile(seq, 1024, 256), _largest_tile(2 * d, 1024, 128),
                    "proj_gates", row0=ga0, cols=2 * d)
    small = _matmul(h, wt, _F32, seq, SMALL_W, "proj_small", row0=k0, cols=SMALL_W)

    oa, lse = [], []
    for group in range(N_GROUPS):
        a_g = _matmul_classmajor(h, wt, lambda j, group=group: (j * N_GROUPS + group) * GROUP_W, 3,
                                 DILATIONS[group], batch, seq, seq, GROUP_W, f"proj_a_g{group}")
        o_g, lse_g = _mixer_a_group(tab, a_g, group, batch, seq)
        oa.append(o_g)
        lse.append(lse_g)

    ob = _mixer_b(tab, idx_k_norm_g[0].reshape(1, IDX_DIM), idx_k_norm_b[0].reshape(1, IDX_DIM),
                  big_b.reshape(batch, seq, -1), small.reshape(batch, seq, -1), tq=256)

    x1 = _merge(x2d, oa, lse, ob.reshape(n, WIDTH_B), gates,
                w_proj_a[0].astype(_BF16), w_proj_b[0].astype(_BF16), w_out[0].astype(_BF16),
                tm=MERGE_TILE)
    y = _mlp(x1, norm_mlp_g[0].reshape(1, d), norm_final_g.reshape(1, d),
             w_mlp_up[0].astype(_BF16), w_mlp_down[0].astype(_BF16),
             tm=_largest_tile(n, 1024, 128), tf=_largest_tile(w_mlp_up.shape[-1], 512, 128))
    return y.reshape(batch, seq, d)
```

```python
import functools
import math

import jax
import jax.numpy as jnp
import numpy as np
from jax import lax
from jax.experimental import pallas as pl
from jax.experimental.pallas import tpu as pltpu

HEAD_DIM = 128
DILATIONS = (1, 4, 16)
N_GROUPS = 3
HEADS_PER_GROUP = 4
HEADS_A = N_GROUPS * HEADS_PER_GROUP
BAND = 128
HEADS_B = 8
N_HEADS = HEADS_A + HEADS_B
IDX_HEADS = 16
IDX_DIM = 64
TOPK = 256
N_BUCKETS = 32
MAX_DISTANCE = 2048
NORM_EPS = 1e-6
NEG_INF = -1e30
INT_MIN = -(2**31)

WIDTH_A = HEADS_A * HEAD_DIM
WIDTH_B = HEADS_B * HEAD_DIM
GROUP_W = HEADS_PER_GROUP * HEAD_DIM
IDX_W = IDX_HEADS * IDX_DIM
SMALL_W = 128
MERGE_TILE = 256

V7X_VMEM_LIMIT = 56 * 1024 * 1024

_F32 = jnp.float32
_BF16 = jnp.bfloat16


def _bucket_bounds():
    n = np.arange(MAX_DISTANCE, dtype=np.int32)
    max_exact = N_BUCKETS // 2
    nf = np.maximum(n, 1).astype(np.float32)
    large = max_exact + (
        np.log(nf / np.float32(max_exact)) / np.float32(math.log(MAX_DISTANCE / max_exact))
        * np.float32(N_BUCKETS - max_exact)
    ).astype(np.int32)
    large = np.minimum(large, N_BUCKETS - 1)
    bucket = np.where(n < max_exact, n, large)
    assert np.all(np.diff(bucket) >= 0)
    bounds = [int(np.argmax(bucket >= b)) if np.any(bucket >= b) else MAX_DISTANCE for b in range(N_BUCKETS)]
    return tuple(bounds)


_BOUNDS = _bucket_bounds()


def _dot_nt(a, b):
    return lax.dot_general(a, b, (((1,), (1,)), ((), ())), preferred_element_type=_F32)


def _bias_from_distance(tab_ref, dist, head):
    val = jnp.full(dist.shape, tab_ref[head], _F32)
    for bkt in range(1, N_BUCKETS):
        val = jnp.where(dist >= _BOUNDS[bkt], tab_ref[bkt * N_HEADS + head], val)
    return val


def _rmsnorm_kernel(x_ref, g_ref, w_ref, h_ref, small_ref):
    x = x_ref[...]
    ms = jnp.mean(x * x, axis=-1, keepdims=True)
    h = (x * lax.rsqrt(ms + NORM_EPS) * g_ref[...]).astype(h_ref.dtype)
    h_ref[...] = h
    small_ref[...] = _dot_nt(h, w_ref[...].astype(h_ref.dtype))


def _rmsnorm(x2d, g, wt, small_row0, tm):
    n, d = x2d.shape
    return pl.pallas_call(
        _rmsnorm_kernel,
        grid=(n // tm,),
        in_specs=[pl.BlockSpec((tm, d), lambda i: (i, 0)), pl.BlockSpec((1, d), lambda i: (0, 0)),
                  pl.BlockSpec((pl.Element(SMALL_W), pl.Element(d)), lambda i: (small_row0, 0))],
        out_specs=[pl.BlockSpec((tm, d), lambda i: (i, 0)), pl.BlockSpec((tm, SMALL_W), lambda i: (i, 0))],
        out_shape=[jax.ShapeDtypeStruct((n, d), _BF16), jax.ShapeDtypeStruct((n, SMALL_W), _F32)],
        compiler_params=pltpu.CompilerParams(
            dimension_semantics=("arbitrary",), vmem_limit_bytes=V7X_VMEM_LIMIT),
        name="rmsnorm_mix",
    )(x2d, g, wt)


def _cast_weight_tile(w_ref, wbf_ref):
    @pl.when(pl.program_id(1) == 0)
    def _():
        wbf_ref[...] = w_ref[...].astype(wbf_ref.dtype)


def _matmul_kernel(h_ref, w_ref, o_ref, wbf_ref):
    _cast_weight_tile(w_ref, wbf_ref)
    o_ref[...] = _dot_nt(h_ref[...], wbf_ref[...]).astype(o_ref.dtype)


def _weight_rows_spec(tn, d, row_start):
    return pl.BlockSpec((pl.Element(tn), pl.Element(d)), lambda j, i: (pl.multiple_of(row_start(j), 8), 0))


def _matmul(h, wt, out_dtype, tm, tn, name, row0, cols):
    n, d = h.shape
    assert cols % tn == 0
    return pl.pallas_call(
        _matmul_kernel,
        grid=(cols // tn, n // tm),
        in_specs=[pl.BlockSpec((tm, d), lambda j, i: (i, 0)),
                  _weight_rows_spec(tn, d, lambda j: row0 + j * tn)],
        out_specs=pl.BlockSpec((tm, tn), lambda j, i: (i, j)),
        out_shape=jax.ShapeDtypeStruct((n, cols), out_dtype),
        scratch_shapes=[pltpu.VMEM((tn, d), _BF16)],
        compiler_params=pltpu.CompilerParams(
            dimension_semantics=("arbitrary", "arbitrary"), vmem_limit_bytes=V7X_VMEM_LIMIT),
        name=name,
    )(h, wt)


def _matmul_classmajor_kernel(h_ref, w_ref, o_ref, acc_ref, tmp_ref, wbf_ref, *, dil, rows):
    _cast_weight_tile(w_ref, wbf_ref)
    res = _dot_nt(h_ref[...], wbf_ref[...])
    if dil == 1:
        o_ref[0, 0] = res.astype(o_ref.dtype)
        return
    for c in range(acc_ref.shape[0]):
        acc_ref[c] = res[:, c * 128:(c + 1) * 128]
    if not _tile_class_major(dil):
        for c in range(acc_ref.shape[0]):
            for r in range(dil):
                o_ref[0, r, :, c * 128:(c + 1) * 128] = (
                    acc_ref[c, pl.ds(r, rows, stride=dil), :].astype(o_ref.dtype))
        return
    s1 = 4
    s2 = dil // s1
    step_rows = rows * s2
    for c in range(acc_ref.shape[0]):
        for r1 in range(s1):
            tmp_ref[c, r1 * step_rows:(r1 + 1) * step_rows, :] = acc_ref[c, pl.ds(r1, step_rows, stride=s1), :]
        for r1 in range(s1):
            for r2 in range(s2):
                o_ref[0, r1 + s1 * r2, :, c * 128:(c + 1) * 128] = (
                    tmp_ref[c, pl.ds(r1 * step_rows + r2, rows, stride=s2), :].astype(o_ref.dtype))


def _matmul_classmajor(h, wt, row_start, n_blocks, dil, batch, seq, tm, tn, name):
    n, d = h.shape
    cols = n_blocks * tn
    tiles_per_seq = seq // tm
    rows = tm // dil
    return pl.pallas_call(
        functools.partial(_matmul_classmajor_kernel, dil=dil, rows=rows),
        grid=(n_blocks, n // tm),
        in_specs=[pl.BlockSpec((tm, d), lambda j, i: (i, 0)), _weight_rows_spec(tn, d, row_start)],
        out_specs=pl.BlockSpec((1, dil, rows, tn),
                               lambda j, i: (i // tiles_per_seq, 0, i % tiles_per_seq, j)),
        out_shape=jax.ShapeDtypeStruct((batch, dil, seq // dil, cols), _BF16),
        scratch_shapes=[pltpu.VMEM((tn // 128, tm, 128), _F32),
                        pltpu.VMEM((tn // 128, tm if _tile_class_major(dil) else 8, 128), _F32),
                        pltpu.VMEM((tn, d), _BF16)],
        compiler_params=pltpu.CompilerParams(
            dimension_semantics=("arbitrary", "arbitrary"), vmem_limit_bytes=V7X_VMEM_LIMIT),
        name=name,
    )(h, wt)


def _tile_class_major(dil):
    return dil % 8 == 0


def _mixer_a_kernel(tab_ref, a_ref, o_ref, lse_ref, bias_ref, *, group, seq):
    dil = DILATIONS[group]
    nq = seq // dil // BAND
    scale = HEAD_DIM ** -0.5

    @pl.when(pl.program_id(0) == 0)
    def _build_bias():
        row = lax.broadcasted_iota(jnp.int32, (BAND, BAND), 0)
        col = lax.broadcasted_iota(jnp.int32, (BAND, BAND), 1)
        for side in range(2):
            j = row - col + BAND * (1 - side)
            valid = jnp.where(j >= 0, j, BAND + 1) <= BAND
            for h in range(HEADS_PER_GROUP):
                val = _bias_from_distance(tab_ref, j * dil, group * HEADS_PER_GROUP + h)
                bias_ref[h, :, side * BAND:(side + 1) * BAND] = jnp.where(valid, val, NEG_INF)

    def key_rows(i):
        return pl.ds(0, BAND) if i == 0 else pl.ds((i - 1) * BAND, 2 * BAND)

    def logits(r, i, h):
        qc = slice(h * HEAD_DIM, (h + 1) * HEAD_DIM)
        kc = slice(GROUP_W + h * HEAD_DIM, GROUP_W + (h + 1) * HEAD_DIM)
        bias = bias_ref[h, :, BAND:] if i == 0 else bias_ref[h]
        return _dot_nt(a_ref[0, r, pl.ds(i * BAND, BAND), qc], a_ref[0, r, key_rows(i), kc]) * scale + bias

    def finish(r, i, h, s):
        m = jnp.max(s, axis=-1, keepdims=True)
        p = jnp.exp(s - m).astype(_BF16)
        v = a_ref[0, r, key_rows(i), 2 * GROUP_W + h * HEAD_DIM:2 * GROUP_W + (h + 1) * HEAD_DIM]
        res = jnp.dot(p, jnp.concatenate([v, jnp.ones_like(v)], axis=-1), preferred_element_type=_F32)
        den = res[:, HEAD_DIM:]
        o_val = res[:, :HEAD_DIM] / den
        lse_val = m + jnp.log(den)
        if not _tile_class_major(dil):
            out_rows = pl.ds(i * BAND, BAND) if dil == 1 else pl.ds(r + i * BAND * dil, BAND, stride=dil)
            o_ref[0, h, out_rows, :] = o_val
            lse_ref[0, h, out_rows, :] = lse_val
        else:
            per = MERGE_TILE // dil
            for c in range(BAND // per):
                rows = pl.ds((i * (BAND // per) + c) * MERGE_TILE + r * per, per)
                o_ref[0, h, rows, :] = o_val[c * per:(c + 1) * per]
                lse_ref[0, h, rows, :] = lse_val[c * per:(c + 1) * per]

    pending = None
    for r in range(dil):
        for i in range(nq):
            cur = [(r, i, h, logits(r, i, h)) for h in range(HEADS_PER_GROUP)]
            if pending is not None:
                for unit in pending:
                    finish(*unit)
            pending = cur
    for unit in pending:
        finish(*unit)


def _mixer_a_group(tab, a_g, group, batch, seq):
    dil = DILATIONS[group]
    out_spec = pl.BlockSpec((1, HEADS_PER_GROUP, seq, HEAD_DIM), lambda b: (b, 0, 0, 0))
    out_sds = jax.ShapeDtypeStruct((batch, HEADS_PER_GROUP, seq, HEAD_DIM), _F32)
    return pl.pallas_call(
        functools.partial(_mixer_a_kernel, group=group, seq=seq),
        grid=(batch,),
        in_specs=[pl.BlockSpec(memory_space=pltpu.SMEM),
                  pl.BlockSpec((1, dil, seq // dil, 3 * GROUP_W), lambda b: (b, 0, 0, 0))],
        out_specs=[out_spec, out_spec],
        out_shape=[out_sds, out_sds],
        scratch_shapes=[pltpu.VMEM((HEADS_PER_GROUP, BAND, 2 * BAND), _F32)],
        compiler_params=pltpu.CompilerParams(
            dimension_semantics=("arbitrary",), vmem_limit_bytes=V7X_VMEM_LIMIT),
        name=f"mixer_a_g{group}",
    )(tab, a_g)


def _mixer_b_kernel(tab_ref, lng_ref, lnb_ref, q_ref, k_ref, v_ref, qidx_ref, smallk_ref, smallq_ref,
                    o_ref, kidx_s, bias_s, key_s, hi_s, lo_s, selb_s, wt_s, s_s, mx_s, mrep_s, acc_s,
                    *, tq, seq):
    b = pl.program_id(0)
    qi = pl.program_id(1)
    nkb = seq // tq
    lane_tiles = tq // 128
    scale = HEAD_DIM ** -0.5
    inv_scale = HEAD_DIM ** 0.5
    neg_scaled = NEG_INF * inv_scale
    row = lax.broadcasted_iota(jnp.int32, (tq, tq), 0)
    col = lax.broadcasted_iota(jnp.int32, (tq, tq), 1)

    @pl.when((b == 0) & (qi == 0))
    def _build_bias():
        def per_delta(dl, carry):
            dist = row - col + dl * tq
            for h in range(HEADS_B):
                bias_s[dl, h] = _bias_from_distance(tab_ref, dist, HEADS_A + h) * inv_scale
            return carry

        lax.fori_loop(0, nkb, per_delta, 0)

    @pl.when(qi == 0)
    def _layer_norm_keys():
        def chunk(c, carry):
            r = pl.multiple_of(c * tq, tq)
            kx = smallk_ref[0, pl.ds(r, tq), :][:, :IDX_DIM]
            mu = jnp.mean(kx, axis=-1, keepdims=True)
            xc = kx - mu
            var = jnp.mean(xc * xc, axis=-1, keepdims=True)
            y = xc * lax.rsqrt(var + NORM_EPS) * lng_ref[...] + lnb_ref[...]
            z = jnp.zeros_like(y)
            kidx_s[0, pl.ds(r, tq), :] = jnp.concatenate([y, z], axis=-1).astype(_BF16)
            kidx_s[1, pl.ds(r, tq), :] = jnp.concatenate([z, y], axis=-1).astype(_BF16)
            return carry

        lax.fori_loop(0, nkb, chunk, 0)

    wt_s[...] = (smallq_ref[0] * (IDX_HEADS ** -0.5 * IDX_DIM ** -0.5)).T

    def score_blk(kj, carry):
        r = pl.multiple_of(kj * tq, tq)
        acc = jnp.zeros((tq, tq), _F32)
        for h in range(IDX_HEADS):
            pair = qidx_ref[0][:, (h // 2) * 128:(h // 2 + 1) * 128]
            x = _dot_nt(kidx_s[h % 2, pl.ds(r, tq), :], pair)
            acc = acc + wt_s[IDX_DIM + h:IDX_DIM + h + 1, :] * jnp.maximum(x, 0.0)
        bits = lax.bitcast_convert_type(acc, jnp.int32)
        key = bits ^ ((bits >> 31) & 0x7FFFFFFF)
        causal = (kj * tq + row) <= (qi * tq + col)
        key = jnp.where(causal, key, INT_MIN)
        key_s[kj] = key
        hi_s[kj] = (key >> 16).astype(jnp.int16)
        lo_s[kj] = ((key & 0xFFFF) - 32768).astype(jnp.int16)
        return carry

    lax.fori_loop(0, qi + 1, score_blk, 0)

    def count_ge(cand):
        def blk(kj, cnt):
            kk = key_s[kj].reshape(tq // 8, 8, tq)
            return cnt + jnp.sum(jnp.where(kk >= cand[None], 1.0, 0.0), axis=0)

        cnt = lax.fori_loop(0, qi + 1, blk, jnp.zeros((8, tq), _F32))
        return jnp.broadcast_to(jnp.sum(cnt, axis=0, keepdims=True), (8, tq))

    one16, zero16, min16 = jnp.ones((), jnp.int16), jnp.zeros((), jnp.int16), jnp.full((), -32768, jnp.int16)

    def packed(cand):
        return jnp.broadcast_to(cand[:1], (16, tq)).astype(jnp.int16)

    def count16(src, cand, strict=False):
        c16 = packed(cand)[None]

        def blk(kj, cnt):
            kk = src[kj].reshape(tq // 16, 16, tq)
            hit = jnp.where((kk > c16) if strict else (kk >= c16), one16, zero16)
            parts = [hit[t] for t in range(tq // 16)]
            while len(parts) > 1:
                parts = [parts[t] + parts[t + 1] for t in range(0, len(parts), 2)]
            return cnt + parts[0]

        cnt = lax.fori_loop(0, qi + 1, blk, jnp.zeros((16, tq), jnp.int16))
        return jnp.broadcast_to(jnp.sum(cnt.astype(_F32), axis=0, keepdims=True), (8, tq))

    def search16(src, target, cnt_floor):
        zero = jnp.zeros((8, tq), jnp.int32)
        cnt0 = count16(src, zero)
        ok0 = cnt0 >= target
        init = (jnp.where(ok0, zero, -32768), jnp.where(ok0, cnt0, cnt_floor))

        def bit_body(bi, carry):
            prefix, cnt_prefix = carry
            cand = prefix | lax.shift_left(jnp.int32(1), 14 - bi)
            cnt = count16(src, cand)
            ok = cnt >= target
            return jnp.where(ok, cand, prefix), jnp.where(ok, cnt, cnt_prefix)

        return lax.fori_loop(0, 15, bit_body, init)

    total = jnp.zeros((8, tq), _F32) + ((qi + 1) * tq).astype(_F32)
    tau_hi, cnt_hi = search16(hi_s, float(TOPK), total)
    above_hi = count16(hi_s, tau_hi, strict=True)
    tau_hi16 = packed(tau_hi)[None]

    def mask_lo(kj, carry):
        hi = hi_s[kj].reshape(tq // 16, 16, tq)
        lo = lo_s[kj].reshape(tq // 16, 16, tq)
        lo_s[kj] = jnp.where(hi == tau_hi16, lo, min16).reshape(tq, tq)
        return carry

    lax.fori_loop(0, qi + 1, mask_lo, 0)
    tau_lo, cnt_lo = search16(lo_s, TOPK - above_hi, cnt_hi - above_hi)
    tau = tau_hi * 65536 + (tau_lo + 32768)
    cnt_tau = above_hi + cnt_lo
    thr = jnp.maximum(tau, INT_MIN + 1)
    has_ties = jnp.max(jnp.where(tau > INT_MIN, cnt_tau, 0.0)) > TOPK

    @pl.when(jnp.logical_not(has_ties))
    def _select_by_threshold():
        def write_sel(kj, carry):
            kk = key_s[kj].reshape(tq // 8, 8, tq)
            sel_t = jnp.where(kk >= thr[None], 0.0, neg_scaled).reshape(tq, tq)
            selb_s[kj] = sel_t.T
            return carry

        lax.fori_loop(0, qi + 1, write_sel, 0)

    @pl.when(has_ties)
    def _select_breaking_ties():
        need = (TOPK - count_ge(thr + 1))[:1]
        earlier = jnp.where(row > col, 1.0, 0.0).astype(_BF16)

        def write_sel(kj, seen):
            kk = key_s[kj].reshape(tq // 8, 8, tq)
            above = jnp.where(kk > thr[None], 1.0, 0.0).reshape(tq, tq)
            equal = jnp.where(kk == thr[None], 1.0, 0.0).reshape(tq, tq)
            rank = seen + jnp.dot(earlier, equal.astype(_BF16), preferred_element_type=_F32)
            take = above + equal * jnp.where(rank < need, 1.0, 0.0)
            selb_s[kj] = jnp.where(take > 0.0, 0.0, neg_scaled).T
            return seen + jnp.sum(equal, axis=0, keepdims=True)

        lax.fori_loop(0, qi + 1, write_sel, jnp.zeros((1, tq), _F32))

    head_group = s_s.shape[0]
    exp2_coeff = scale * math.log2(math.e)
    for h0 in range(0, HEADS_B, head_group):
        for hl in range(head_group):
            mx_s[hl] = jnp.full((tq, 128), -3e38, _F32)

        def logits_blk(kj, carry, h0=h0):
            r = pl.multiple_of(kj * tq, tq)
            for hl in range(head_group):
                h = h0 + hl
                cs = slice(h * HEAD_DIM, (h + 1) * HEAD_DIM)
                s = _dot_nt(q_ref[0][:, cs], k_ref[0, pl.ds(r, tq), cs]) + bias_s[qi - kj, h] + selb_s[kj]
                s_s[hl, kj] = s
                part = s[:, :128]
                for c in range(1, lane_tiles):
                    part = jnp.maximum(part, s[:, c * 128:(c + 1) * 128])
                mx_s[hl] = jnp.maximum(mx_s[hl], part)
            return carry

        lax.fori_loop(0, qi + 1, logits_blk, 0)
        for hl in range(head_group):
            m = jnp.max(mx_s[hl], axis=-1, keepdims=True)
            mrep_s[hl] = jnp.broadcast_to(m, (tq, 128))
            acc_s[hl] = jnp.zeros((tq, 2 * HEAD_DIM), _F32)

        def pv_blk(kj, carry, h0=h0):
            r = pl.multiple_of(kj * tq, tq)
            for hl in range(head_group):
                h = h0 + hl
                cs = slice(h * HEAD_DIM, (h + 1) * HEAD_DIM)
                m = mrep_s[hl]
                p = jnp.exp2((s_s[hl, kj] - jnp.concatenate([m] * lane_tiles, axis=-1)) * exp2_coeff)
                v = v_ref[0, pl.ds(r, tq), cs]
                acc_s[hl] = acc_s[hl] + jnp.dot(p.astype(_BF16), jnp.concatenate([v, jnp.ones_like(v)], axis=-1),
                                                preferred_element_type=_F32)
            return carry

        lax.fori_loop(0, qi + 1, pv_blk, 0)
        for hl in range(head_group):
            h = h0 + hl
            acc = acc_s[hl]
            o_ref[0, :, h * HEAD_DIM:(h + 1) * HEAD_DIM] = (acc[:, :HEAD_DIM] / acc[:, HEAD_DIM:]).astype(o_ref.dtype)


def _mixer_b(tab, ln_g, ln_b, big3, small3, tq):
    batch, seq, _ = big3.shape
    nkb = seq // tq
    head_group = 4
    full = lambda cb: pl.BlockSpec((1, seq, WIDTH_B), lambda b, i, cb=cb: (b, 0, cb),
                                   pipeline_mode=pl.Buffered(1))
    tile = lambda cb: pl.BlockSpec((1, tq, WIDTH_B), lambda b, i, cb=cb: (b, i, cb))
    small_cb = small3.shape[-1] // SMALL_W - 1
    return pl.pallas_call(
        functools.partial(_mixer_b_kernel, tq=tq, seq=seq),
        grid=(batch, nkb),
        in_specs=[
            pl.BlockSpec(memory_space=pltpu.SMEM),
            pl.BlockSpec((1, IDX_DIM), lambda b, i: (0, 0)),
            pl.BlockSpec((1, IDX_DIM), lambda b, i: (0, 0)),
            tile(0), full(1), full(2), tile(3),
            pl.BlockSpec((1, seq, SMALL_W), lambda b, i: (b, 0, small_cb)),
            pl.BlockSpec((1, tq, SMALL_W), lambda b, i: (b, i, small_cb)),
        ],
        out_specs=pl.BlockSpec((1, tq, WIDTH_B), lambda b, i: (b, i, 0)),
        out_shape=jax.ShapeDtypeStruct((batch, seq, WIDTH_B), _BF16),
        scratch_shapes=[
            pltpu.VMEM((2, seq, 2 * IDX_DIM), _BF16),
            pltpu.VMEM((nkb, HEADS_B, tq, tq), _F32),
            pltpu.VMEM((nkb, tq, tq), jnp.int32),
            pltpu.VMEM((nkb, tq, tq), jnp.int16),
            pltpu.VMEM((nkb, tq, tq), jnp.int16),
            pltpu.VMEM((nkb, tq, tq), _F32),
            pltpu.VMEM((SMALL_W, tq), _F32),
            pltpu.VMEM((head_group, nkb, tq, tq), _F32),
            pltpu.VMEM((head_group, tq, 128), _F32),
            pltpu.VMEM((head_group, tq, 128), _F32),
            pltpu.VMEM((head_group, tq, 2 * HEAD_DIM), _F32),
        ],
        compiler_params=pltpu.CompilerParams(
            dimension_semantics=("arbitrary", "arbitrary"), vmem_limit_bytes=V7X_VMEM_LIMIT),
        name="mixer_b",
    )(tab, ln_g, ln_b, big3, big3, big3, big3, small3, small3)


def _merge_kernel(x_ref, oa0_ref, oa1_ref, oa2_ref, l0_ref, l1_ref, l2_ref, ob_ref, ga_ref, gb_ref,
                  wpa_ref, wpb_ref, wout_ref, wu_ref, wd_ref, y_ref, wu_bf_ref, wd_bf_ref):
    wu_bf_ref[...] = wu_ref[...].astype(wu_bf_ref.dtype)
    wd_bf_ref[...] = wd_ref[...].astype(wd_bf_ref.dtype)

    def natural(ref, h, group):
        dil = DILATIONS[group]
        if not _tile_class_major(dil):
            return ref[0, h]
        per = MERGE_TILE // dil
        return jnp.concatenate([ref[0, h, pl.ds(slot, dil, stride=per), :] for slot in range(per)], axis=0)

    pb = jnp.dot(ob_ref[...], wpb_ref[...], preferred_element_type=_F32)
    heads = []
    for h in range(HEADS_PER_GROUP):
        l0, l1, l2 = natural(l0_ref, h, 0), natural(l1_ref, h, 1), natural(l2_ref, h, 2)
        m = jnp.maximum(jnp.maximum(l0, l1), l2)
        e0, e1, e2 = jnp.exp(l0 - m), jnp.exp(l1 - m), jnp.exp(l2 - m)
        heads.append((e0 * natural(oa0_ref, h, 0) + e1 * natural(oa1_ref, h, 1) + e2 * natural(oa2_ref, h, 2))
                     / (e0 + e1 + e2))
    o_a = jnp.concatenate(heads, axis=-1)
    pa = jnp.dot(o_a.astype(_BF16), wpa_ref[...], preferred_element_type=_F32)
    merged = jax.nn.sigmoid(ga_ref[...]) * pa + jax.nn.sigmoid(gb_ref[...]) * pb
    y_ref[...] = x_ref[...] + jnp.dot(merged.astype(_BF16), wout_ref[...], preferred_element_type=_F32)


def _merge(x2d, oa, lse, ob2d, fout, wpa, wpb, wout, w_up, w_down, tm):
    n, d = x2d.shape
    steps = n // tm
    dff = w_up.shape[1]
    ff = dff // steps
    assert dff % steps == 0 and ff % 128 == 0
    tiles_per_seq = oa[0].shape[2] // tm
    rows = lambda w, cb=0: pl.BlockSpec((tm, w), lambda i, cb=cb: (i, cb))
    heads = pl.BlockSpec((1, HEADS_PER_GROUP, tm, HEAD_DIM),
                         lambda i: (i // tiles_per_seq, 0, i % tiles_per_seq, 0))
    const = lambda shape: pl.BlockSpec(shape, lambda i: (0, 0), pipeline_mode=pl.Buffered(1))
    up_slice = pl.BlockSpec((d, ff), lambda i: (0, i))
    down_slice = pl.BlockSpec((ff, d), lambda i: (i, 0))
    return pl.pallas_call(
        _merge_kernel,
        grid=(steps,),
        in_specs=[rows(d), heads, heads, heads, heads, heads, heads, rows(WIDTH_B), rows(d, 0), rows(d, 1),
                  const(wpa.shape), const(wpb.shape), const(wout.shape), up_slice, down_slice],
        out_specs=[rows(d), up_slice, down_slice],
        out_shape=[jax.ShapeDtypeStruct((n, d), _F32), jax.ShapeDtypeStruct(w_up.shape, _BF16),
                   jax.ShapeDtypeStruct(w_down.shape, _BF16)],
        compiler_params=pltpu.CompilerParams(
            dimension_semantics=("arbitrary",), vmem_limit_bytes=V7X_VMEM_LIMIT),
        name="merge_out_proj",
    )(x2d, oa[0], oa[1], oa[2], lse[0], lse[1], lse[2], ob2d, fout, fout, wpa, wpb, wout, w_up, w_down)


def _mlp_kernel(x_ref, g_ref, gf_ref, wu_ref, wd_ref, y_ref, h_ref):
    j = pl.program_id(1)

    @pl.when(j == 0)
    def _():
        x = x_ref[...]
        ms = jnp.mean(x * x, axis=-1, keepdims=True)
        h_ref[...] = (x * lax.rsqrt(ms + NORM_EPS) * g_ref[...]).astype(_BF16)
        y_ref[...] = x

    u = jnp.maximum(jnp.dot(h_ref[...], wu_ref[...], preferred_element_type=_F32), 0.0)
    y_ref[...] += jnp.dot((u * u).astype(_BF16), wd_ref[...], preferred_element_type=_F32)

    @pl.when(j == pl.num_programs(1) - 1)
    def _():
        y = y_ref[...]
        ms = jnp.mean(y * y, axis=-1, keepdims=True)
        y_ref[...] = y * lax.rsqrt(ms + NORM_EPS) * gf_ref[...]


def _mlp(x2d, g, gf, wu, wd, tm, tf):
    n, d = x2d.shape
    dff = wu.shape[1]
    return pl.pallas_call(
        _mlp_kernel,
        grid=(n // tm, dff // tf),
        in_specs=[
            pl.BlockSpec((tm, d), lambda i, j: (i, 0)),
            pl.BlockSpec((1, d), lambda i, j: (0, 0)),
            pl.BlockSpec((1, d), lambda i, j: (0, 0)),
            pl.BlockSpec((d, tf), lambda i, j: (0, j)),
            pl.BlockSpec((tf, d), lambda i, j: (j, 0)),
        ],
        out_specs=pl.BlockSpec((tm, d), lambda i, j: (i, 0)),
        out_shape=jax.ShapeDtypeStruct((n, d), _F32),
        scratch_shapes=[pltpu.VMEM((tm, d), _BF16)],
        compiler_params=pltpu.CompilerParams(
            dimension_semantics=("arbitrary", "arbitrary"), vmem_limit_bytes=V7X_VMEM_LIMIT),
        name="mlp_final_norm",
    )(x2d, g, gf, wu, wd)


def _largest_tile(n, limit, step):
    t = (min(n, limit) // step) * step
    while n % t:
        t -= step
    return t


def kernel(x, norm_mix_g, w_in, idx_k_norm_g, idx_k_norm_b, rel_bias_table, w_proj_a, w_proj_b, w_out,
           norm_mlp_g, w_mlp_up, w_mlp_down, norm_final_g):
    batch, seq, d = x.shape
    assert seq == MAX_DISTANCE and seq % (DILATIONS[-1] * BAND) == 0
    assert w_in.shape[0] == 1, "single layer"
    n = batch * seq
    x2d = x.reshape(n, d)
    tab = rel_bias_table.reshape(-1)

    b0 = 3 * WIDTH_A
    k0 = b0 + 3 * WIDTH_B + IDX_W
    ga0 = k0 + IDX_DIM + IDX_HEADS
    wt = w_in[0].T

    h, small = _rmsnorm(x2d, norm_mix_g[0].reshape(1, d), wt, k0, _largest_tile(n, 1024, 8))
    big_b = _matmul(h, wt, _BF16, seq, GROUP_W, "proj_b_idx", row0=b0, cols=k0 - b0)
    gates = _matmul(h, wt, _F32, _largest_tile(seq, 1024, 256), _largest_tile(2 * d, 1024, 128),
                    "proj_gates", row0=ga0, cols=2 * d)

    oa, lse = [], []
    for group in range(N_GROUPS):
        a_g = _matmul_classmajor(h, wt, lambda j, group=group: (j * N_GROUPS + group) * GROUP_W, 3,
                                 DILATIONS[group], batch, seq, seq, GROUP_W, f"proj_a_g{group}")
        o_g, lse_g = _mixer_a_group(tab, a_g, group, batch, seq)
        oa.append(o_g)
        lse.append(lse_g)

    ob = _mixer_b(tab, idx_k_norm_g[0].reshape(1, IDX_DIM), idx_k_norm_b[0].reshape(1, IDX_DIM),
                  big_b.reshape(batch, seq, -1), small.reshape(batch, seq, -1), tq=256)

    x1, w_up, w_down = _merge(x2d, oa, lse, ob.reshape(n, WIDTH_B), gates,
                              w_proj_a[0].astype(_BF16), w_proj_b[0].astype(_BF16), w_out[0].astype(_BF16),
                              w_mlp_up[0], w_mlp_down[0], tm=MERGE_TILE)
    y = _mlp(x1, norm_mlp_g[0].reshape(1, d), norm_final_g.reshape(1, d), w_up, w_down,
             tm=_largest_tile(n, 1024, 128), tf=_largest_tile(w_mlp_up.shape[-1], 512, 128))
    return y.reshape(batch, seq, d)
```

```python
import functools
import math

import jax
import jax.numpy as jnp
import numpy as np
from jax import lax
from jax.experimental import pallas as pl
from jax.experimental.pallas import tpu as pltpu

HEAD_DIM = 128
DILATIONS = (1, 4, 16)
N_GROUPS = 3
HEADS_PER_GROUP = 4
HEADS_A = N_GROUPS * HEADS_PER_GROUP
BAND = 128
HEADS_B = 8
N_HEADS = HEADS_A + HEADS_B
IDX_HEADS = 16
IDX_DIM = 64
TOPK = 256
N_BUCKETS = 32
MAX_DISTANCE = 2048
NORM_EPS = 1e-6
NEG_INF = -1e30
INT_MIN = -(2**31)

WIDTH_A = HEADS_A * HEAD_DIM
WIDTH_B = HEADS_B * HEAD_DIM
GROUP_W = HEADS_PER_GROUP * HEAD_DIM
IDX_W = IDX_HEADS * IDX_DIM
SMALL_W = 128
MERGE_TILE = 256

V7X_VMEM_LIMIT = 56 * 1024 * 1024

_F32 = jnp.float32
_BF16 = jnp.bfloat16


def _bucket_bounds():
    n = np.arange(MAX_DISTANCE, dtype=np.int32)
    max_exact = N_BUCKETS // 2
    nf = np.maximum(n, 1).astype(np.float32)
    large = max_exact + (
        np.log(nf / np.float32(max_exact)) / np.float32(math.log(MAX_DISTANCE / max_exact))
        * np.float32(N_BUCKETS - max_exact)
    ).astype(np.int32)
    large = np.minimum(large, N_BUCKETS - 1)
    bucket = np.where(n < max_exact, n, large)
    assert np.all(np.diff(bucket) >= 0)
    bounds = [int(np.argmax(bucket >= b)) if np.any(bucket >= b) else MAX_DISTANCE for b in range(N_BUCKETS)]
    return tuple(bounds)


_BOUNDS = _bucket_bounds()


def _dot_nt(a, b):
    return lax.dot_general(a, b, (((1,), (1,)), ((), ())), preferred_element_type=_F32)


def _bias_from_distance(tab_ref, dist, head):
    val = jnp.full(dist.shape, tab_ref[head], _F32)
    for bkt in range(1, N_BUCKETS):
        val = jnp.where(dist >= _BOUNDS[bkt], tab_ref[bkt * N_HEADS + head], val)
    return val


def _rmsnorm_kernel(x_ref, g_ref, w_ref, h_ref, small_ref):
    x = x_ref[...]
    ms = jnp.mean(x * x, axis=-1, keepdims=True)
    h = (x * lax.rsqrt(ms + NORM_EPS) * g_ref[...]).astype(h_ref.dtype)
    h_ref[...] = h
    small_ref[...] = _dot_nt(h, w_ref[...].astype(h_ref.dtype))


def _rmsnorm(x2d, g, wt, small_row0, tm):
    n, d = x2d.shape
    return pl.pallas_call(
        _rmsnorm_kernel,
        grid=(n // tm,),
        in_specs=[pl.BlockSpec((tm, d), lambda i: (i, 0)), pl.BlockSpec((1, d), lambda i: (0, 0)),
                  pl.BlockSpec((pl.Element(SMALL_W), pl.Element(d)), lambda i: (small_row0, 0))],
        out_specs=[pl.BlockSpec((tm, d), lambda i: (i, 0)), pl.BlockSpec((tm, SMALL_W), lambda i: (i, 0))],
        out_shape=[jax.ShapeDtypeStruct((n, d), _BF16), jax.ShapeDtypeStruct((n, SMALL_W), _F32)],
        compiler_params=pltpu.CompilerParams(
            dimension_semantics=("arbitrary",), vmem_limit_bytes=V7X_VMEM_LIMIT),
        name="rmsnorm_mix",
    )(x2d, g, wt)


def _cast_weight_tile(w_ref, wbf_ref):
    @pl.when(pl.program_id(1) == 0)
    def _():
        wbf_ref[...] = w_ref[...].astype(wbf_ref.dtype)


def _matmul_kernel(h_ref, w_ref, o_ref, wbf_ref):
    _cast_weight_tile(w_ref, wbf_ref)
    o_ref[...] = _dot_nt(h_ref[...], wbf_ref[...]).astype(o_ref.dtype)


def _weight_rows_spec(tn, d, row_start):
    return pl.BlockSpec((pl.Element(tn), pl.Element(d)), lambda j, i: (pl.multiple_of(row_start(j), 8), 0))


def _matmul_and_round_kernel(h_ref, w_ref, wu_ref, wd_ref, o_ref, wu_bf_ref, wd_bf_ref, wbf_ref):
    wu_bf_ref[...] = wu_ref[...].astype(wu_bf_ref.dtype)
    wd_bf_ref[...] = wd_ref[...].astype(wd_bf_ref.dtype)
    _matmul_kernel(h_ref, w_ref, o_ref, wbf_ref)


def _matmul(h, wt, out_dtype, tm, tn, name, row0, cols, round_too=None):
    n, d = h.shape
    assert cols % tn == 0
    n_i = n // tm
    steps = (cols // tn) * n_i
    in_specs = [pl.BlockSpec((tm, d), lambda j, i: (i, 0)), _weight_rows_spec(tn, d, lambda j: row0 + j * tn)]
    out_specs = [pl.BlockSpec((tm, tn), lambda j, i: (i, j))]
    out_shape = [jax.ShapeDtypeStruct((n, cols), out_dtype)]
    operands = [h, wt]
    kernel_fn = _matmul_kernel
    if round_too is not None:
        w_up, w_down = round_too
        ff = w_up.shape[1] // steps
        assert w_up.shape[1] % steps == 0 and ff % 128 == 0
        slices = [pl.BlockSpec((w_up.shape[0], ff), lambda j, i: (0, j * n_i + i)),
                  pl.BlockSpec((ff, w_down.shape[1]), lambda j, i: (j * n_i + i, 0))]
        in_specs += slices
        out_specs += slices
        out_shape += [jax.ShapeDtypeStruct(w_up.shape, _BF16), jax.ShapeDtypeStruct(w_down.shape, _BF16)]
        operands += [w_up, w_down]
        kernel_fn = _matmul_and_round_kernel
    out = pl.pallas_call(
        kernel_fn,
        grid=(cols // tn, n_i),
        in_specs=in_specs,
        out_specs=out_specs,
        out_shape=out_shape,
        scratch_shapes=[pltpu.VMEM((tn, d), _BF16)],
        compiler_params=pltpu.CompilerParams(
            dimension_semantics=("arbitrary", "arbitrary"), vmem_limit_bytes=V7X_VMEM_LIMIT),
        name=name,
    )(*operands)
    return out[0] if round_too is None else out


def _matmul_classmajor_kernel(h_ref, w_ref, o_ref, acc_ref, tmp_ref, wbf_ref, *, dil, rows):
    _cast_weight_tile(w_ref, wbf_ref)
    res = _dot_nt(h_ref[...], wbf_ref[...])
    if dil == 1:
        o_ref[0, 0] = res.astype(o_ref.dtype)
        return
    for c in range(acc_ref.shape[0]):
        acc_ref[c] = res[:, c * 128:(c + 1) * 128]
    if not _tile_class_major(dil):
        for c in range(acc_ref.shape[0]):
            for r in range(dil):
                o_ref[0, r, :, c * 128:(c + 1) * 128] = (
                    acc_ref[c, pl.ds(r, rows, stride=dil), :].astype(o_ref.dtype))
        return
    s1 = 4
    s2 = dil // s1
    step_rows = rows * s2
    for c in range(acc_ref.shape[0]):
        for r1 in range(s1):
            tmp_ref[c, r1 * step_rows:(r1 + 1) * step_rows, :] = acc_ref[c, pl.ds(r1, step_rows, stride=s1), :]
        for r1 in range(s1):
            for r2 in range(s2):
                o_ref[0, r1 + s1 * r2, :, c * 128:(c + 1) * 128] = (
                    tmp_ref[c, pl.ds(r1 * step_rows + r2, rows, stride=s2), :].astype(o_ref.dtype))


def _matmul_classmajor(h, wt, row_start, n_blocks, dil, batch, seq, tm, tn, name):
    n, d = h.shape
    cols = n_blocks * tn
    tiles_per_seq = seq // tm
    rows = tm // dil
    return pl.pallas_call(
        functools.partial(_matmul_classmajor_kernel, dil=dil, rows=rows),
        grid=(n_blocks, n // tm),
        in_specs=[pl.BlockSpec((tm, d), lambda j, i: (i, 0)), _weight_rows_spec(tn, d, row_start)],
        out_specs=pl.BlockSpec((1, dil, rows, tn),
                               lambda j, i: (i // tiles_per_seq, 0, i % tiles_per_seq, j)),
        out_shape=jax.ShapeDtypeStruct((batch, dil, seq // dil, cols), _BF16),
        scratch_shapes=[pltpu.VMEM((tn // 128, tm, 128), _F32),
                        pltpu.VMEM((tn // 128, tm if _tile_class_major(dil) else 8, 128), _F32),
                        pltpu.VMEM((tn, d), _BF16)],
        compiler_params=pltpu.CompilerParams(
            dimension_semantics=("arbitrary", "arbitrary"), vmem_limit_bytes=V7X_VMEM_LIMIT),
        name=name,
    )(h, wt)


def _tile_class_major(dil):
    return dil % 8 == 0


def _mixer_a_kernel(tab_ref, a_ref, o_ref, lse_ref, bias_ref, *, group, seq):
    dil = DILATIONS[group]
    nq = seq // dil // BAND
    scale = HEAD_DIM ** -0.5

    @pl.when(pl.program_id(0) == 0)
    def _build_bias():
        row = lax.broadcasted_iota(jnp.int32, (BAND, BAND), 0)
        col = lax.broadcasted_iota(jnp.int32, (BAND, BAND), 1)
        for side in range(2):
            j = row - col + BAND * (1 - side)
            valid = jnp.where(j >= 0, j, BAND + 1) <= BAND
            for h in range(HEADS_PER_GROUP):
                val = _bias_from_distance(tab_ref, j * dil, group * HEADS_PER_GROUP + h)
                bias_ref[h, :, side * BAND:(side + 1) * BAND] = jnp.where(valid, val, NEG_INF)

    def key_rows(i):
        return pl.ds(0, BAND) if i == 0 else pl.ds((i - 1) * BAND, 2 * BAND)

    def logits(r, i, h):
        qc = slice(h * HEAD_DIM, (h + 1) * HEAD_DIM)
        kc = slice(GROUP_W + h * HEAD_DIM, GROUP_W + (h + 1) * HEAD_DIM)
        bias = bias_ref[h, :, BAND:] if i == 0 else bias_ref[h]
        return _dot_nt(a_ref[0, r, pl.ds(i * BAND, BAND), qc], a_ref[0, r, key_rows(i), kc]) * scale + bias

    def finish(r, i, h, s):
        m = jnp.max(s, axis=-1, keepdims=True)
        p = jnp.exp(s - m).astype(_BF16)
        v = a_ref[0, r, key_rows(i), 2 * GROUP_W + h * HEAD_DIM:2 * GROUP_W + (h + 1) * HEAD_DIM]
        res = jnp.dot(p, jnp.concatenate([v, jnp.ones_like(v)], axis=-1), preferred_element_type=_F32)
        den = res[:, HEAD_DIM:]
        o_val = res[:, :HEAD_DIM] / den
        lse_val = m + jnp.log(den)
        if not _tile_class_major(dil):
            out_rows = pl.ds(i * BAND, BAND) if dil == 1 else pl.ds(r + i * BAND * dil, BAND, stride=dil)
            o_ref[0, h, out_rows, :] = o_val
            lse_ref[0, h, out_rows, :] = lse_val
        else:
            per = MERGE_TILE // dil
            for c in range(BAND // per):
                rows = pl.ds((i * (BAND // per) + c) * MERGE_TILE + r * per, per)
                o_ref[0, h, rows, :] = o_val[c * per:(c + 1) * per]
                lse_ref[0, h, rows, :] = lse_val[c * per:(c + 1) * per]

    pending = None
    for r in range(dil):
        for i in range(nq):
            cur = [(r, i, h, logits(r, i, h)) for h in range(HEADS_PER_GROUP)]
            if pending is not None:
                for unit in pending:
                    finish(*unit)
            pending = cur
    for unit in pending:
        finish(*unit)


def _mixer_a_group(tab, a_g, group, batch, seq):
    dil = DILATIONS[group]
    out_spec = pl.BlockSpec((1, HEADS_PER_GROUP, seq, HEAD_DIM), lambda b: (b, 0, 0, 0))
    out_sds = jax.ShapeDtypeStruct((batch, HEADS_PER_GROUP, seq, HEAD_DIM), _F32)
    return pl.pallas_call(
        functools.partial(_mixer_a_kernel, group=group, seq=seq),
        grid=(batch,),
        in_specs=[pl.BlockSpec(memory_space=pltpu.SMEM),
                  pl.BlockSpec((1, dil, seq // dil, 3 * GROUP_W), lambda b: (b, 0, 0, 0))],
        out_specs=[out_spec, out_spec],
        out_shape=[out_sds, out_sds],
        scratch_shapes=[pltpu.VMEM((HEADS_PER_GROUP, BAND, 2 * BAND), _F32)],
        compiler_params=pltpu.CompilerParams(
            dimension_semantics=("arbitrary",), vmem_limit_bytes=V7X_VMEM_LIMIT),
        name=f"mixer_a_g{group}",
    )(tab, a_g)


def _mixer_b_kernel(tab_ref, lng_ref, lnb_ref, q_ref, k_ref, v_ref, qidx_ref, smallk_ref, smallq_ref,
                    o_ref, kidx_s, bias_s, key_s, hi_s, lo_s, selb_s, wt_s, s_s, mx_s, mrep_s, acc_s,
                    *, tq, seq):
    b = pl.program_id(0)
    qi = pl.program_id(1)
    nkb = seq // tq
    lane_tiles = tq // 128
    scale = HEAD_DIM ** -0.5
    inv_scale = HEAD_DIM ** 0.5
    neg_scaled = NEG_INF * inv_scale
    row = lax.broadcasted_iota(jnp.int32, (tq, tq), 0)
    col = lax.broadcasted_iota(jnp.int32, (tq, tq), 1)

    @pl.when((b == 0) & (qi == 0))
    def _build_bias():
        def per_delta(dl, carry):
            dist = row - col + dl * tq
            for h in range(HEADS_B):
                bias_s[dl, h] = _bias_from_distance(tab_ref, dist, HEADS_A + h) * inv_scale
            return carry

        lax.fori_loop(0, nkb, per_delta, 0)

    @pl.when(qi == 0)
    def _layer_norm_keys():
        def chunk(c, carry):
            r = pl.multiple_of(c * tq, tq)
            kx = smallk_ref[0, pl.ds(r, tq), :][:, :IDX_DIM]
            mu = jnp.mean(kx, axis=-1, keepdims=True)
            xc = kx - mu
            var = jnp.mean(xc * xc, axis=-1, keepdims=True)
            y = xc * lax.rsqrt(var + NORM_EPS) * lng_ref[...] + lnb_ref[...]
            z = jnp.zeros_like(y)
            kidx_s[0, pl.ds(r, tq), :] = jnp.concatenate([y, z], axis=-1).astype(_BF16)
            kidx_s[1, pl.ds(r, tq), :] = jnp.concatenate([z, y], axis=-1).astype(_BF16)
            return carry

        lax.fori_loop(0, nkb, chunk, 0)

    wt_s[...] = (smallq_ref[0] * (IDX_HEADS ** -0.5 * IDX_DIM ** -0.5)).T

    def score_blk(kj, carry):
        r = pl.multiple_of(kj * tq, tq)
        acc = jnp.zeros((tq, tq), _F32)
        for h in range(IDX_HEADS):
            pair = qidx_ref[0][:, (h // 2) * 128:(h // 2 + 1) * 128]
            x = _dot_nt(kidx_s[h % 2, pl.ds(r, tq), :], pair)
            acc = acc + wt_s[IDX_DIM + h:IDX_DIM + h + 1, :] * jnp.maximum(x, 0.0)
        bits = lax.bitcast_convert_type(acc, jnp.int32)
        key = bits ^ ((bits >> 31) & 0x7FFFFFFF)
        causal = (kj * tq + row) <= (qi * tq + col)
        key = jnp.where(causal, key, INT_MIN)
        key_s[kj] = key
        hi_s[kj] = (key >> 16).astype(jnp.int16)
        lo_s[kj] = ((key & 0xFFFF) - 32768).astype(jnp.int16)
        return carry

    lax.fori_loop(0, qi + 1, score_blk, 0)

    def count_ge(cand):
        def blk(kj, cnt):
            kk = key_s[kj].reshape(tq // 8, 8, tq)
            return cnt + jnp.sum(jnp.where(kk >= cand[None], 1.0, 0.0), axis=0)

        cnt = lax.fori_loop(0, qi + 1, blk, jnp.zeros((8, tq), _F32))
        return jnp.broadcast_to(jnp.sum(cnt, axis=0, keepdims=True), (8, tq))

    one16, zero16, min16 = jnp.ones((), jnp.int16), jnp.zeros((), jnp.int16), jnp.full((), -32768, jnp.int16)

    def packed(cand):
        return jnp.broadcast_to(cand[:1], (16, tq)).astype(jnp.int16)

    def count16(src, cand, strict=False):
        c16 = packed(cand)[None]

        def blk(kj, cnt):
            kk = src[kj].reshape(tq // 16, 16, tq)
            hit = jnp.where((kk > c16) if strict else (kk >= c16), one16, zero16)
            parts = [hit[t] for t in range(tq // 16)]
            while len(parts) > 1:
                parts = [parts[t] + parts[t + 1] for t in range(0, len(parts), 2)]
            return cnt + parts[0]

        cnt = lax.fori_loop(0, qi + 1, blk, jnp.zeros((16, tq), jnp.int16))
        return jnp.broadcast_to(jnp.sum(cnt.astype(_F32), axis=0, keepdims=True), (8, tq))

    def search16(src, target, cnt_floor):
        zero = jnp.zeros((8, tq), jnp.int32)
        cnt0 = count16(src, zero)
        ok0 = cnt0 >= target
        init = (jnp.where(ok0, zero, -32768), jnp.where(ok0, cnt0, cnt_floor))

        def bit_body(bi, carry):
            prefix, cnt_prefix = carry
            cand = prefix | lax.shift_left(jnp.int32(1), 14 - bi)
            cnt = count16(src, cand)
            ok = cnt >= target
            return jnp.where(ok, cand, prefix), jnp.where(ok, cnt, cnt_prefix)

        return lax.fori_loop(0, 15, bit_body, init)

    def kth_largest_key():
        total = jnp.zeros((8, tq), _F32) + ((qi + 1) * tq).astype(_F32)
        tau_hi, cnt_hi = search16(hi_s, float(TOPK), total)
        above_hi = count16(hi_s, tau_hi, strict=True)
        tau_hi16 = packed(tau_hi)[None]

        def mask_lo(kj, carry):
            hi = hi_s[kj].reshape(tq // 16, 16, tq)
            lo = lo_s[kj].reshape(tq // 16, 16, tq)
            lo_s[kj] = jnp.where(hi == tau_hi16, lo, min16).reshape(tq, tq)
            return carry

        lax.fori_loop(0, qi + 1, mask_lo, 0)
        tau_lo, cnt_lo = search16(lo_s, TOPK - above_hi, cnt_hi - above_hi)
        return tau_hi * 65536 + (tau_lo + 32768), above_hi + cnt_lo

    def every_key():
        return jnp.full((8, tq), INT_MIN, jnp.int32), jnp.zeros((8, tq), _F32)

    tau, cnt_tau = lax.cond((qi + 1) * tq > TOPK, kth_largest_key, every_key)
    thr = jnp.maximum(tau, INT_MIN + 1)
    has_ties = jnp.max(jnp.where(tau > INT_MIN, cnt_tau, 0.0)) > TOPK

    @pl.when(jnp.logical_not(has_ties))
    def _select_by_threshold():
        def write_sel(kj, carry):
            kk = key_s[kj].reshape(tq // 8, 8, tq)
            sel_t = jnp.where(kk >= thr[None], 0.0, neg_scaled).reshape(tq, tq)
            selb_s[kj] = sel_t.T
            return carry

        lax.fori_loop(0, qi + 1, write_sel, 0)

    @pl.when(has_ties)
    def _select_breaking_ties():
        need = (TOPK - count_ge(thr + 1))[:1]
        earlier = jnp.where(row > col, 1.0, 0.0).astype(_BF16)

        def write_sel(kj, seen):
            kk = key_s[kj].reshape(tq // 8, 8, tq)
            above = jnp.where(kk > thr[None], 1.0, 0.0).reshape(tq, tq)
            equal = jnp.where(kk == thr[None], 1.0, 0.0).reshape(tq, tq)
            rank = seen + jnp.dot(earlier, equal.astype(_BF16), preferred_element_type=_F32)
            take = above + equal * jnp.where(rank < need, 1.0, 0.0)
            selb_s[kj] = jnp.where(take > 0.0, 0.0, neg_scaled).T
            return seen + jnp.sum(equal, axis=0, keepdims=True)

        lax.fori_loop(0, qi + 1, write_sel, jnp.zeros((1, tq), _F32))

    n_groups, head_group = s_s.shape[:2]
    exp2_coeff = scale * math.log2(math.e)

    def logits_pass(kj, g):
        r = pl.multiple_of(kj * tq, tq)
        for hl in range(head_group):
            h = g * head_group + hl
            cs = slice(h * HEAD_DIM, (h + 1) * HEAD_DIM)
            s = _dot_nt(q_ref[0][:, cs], k_ref[0, pl.ds(r, tq), cs]) + bias_s[qi - kj, h] + selb_s[kj]
            s_s[g, hl, kj] = s
            part = s[:, :128]
            for c in range(1, lane_tiles):
                part = jnp.maximum(part, s[:, c * 128:(c + 1) * 128])
            mx_s[g, hl] = jnp.maximum(mx_s[g, hl], part)

    def value_pass(kj, g):
        r = pl.multiple_of(kj * tq, tq)
        for hl in range(head_group):
            h = g * head_group + hl
            m = mrep_s[g, hl]
            p = jnp.exp2((s_s[g, hl, kj] - jnp.concatenate([m] * lane_tiles, axis=-1)) * exp2_coeff)
            v = v_ref[0, pl.ds(r, tq), h * HEAD_DIM:(h + 1) * HEAD_DIM]
            acc_s[g, hl] = acc_s[g, hl] + jnp.dot(
                p.astype(_BF16), jnp.concatenate([v, jnp.ones_like(v)], axis=-1), preferred_element_type=_F32)

    def start_group(g):
        for hl in range(head_group):
            mx_s[g, hl] = jnp.full((tq, 128), -3e38, _F32)

    def close_logits(g):
        for hl in range(head_group):
            m = jnp.max(mx_s[g, hl], axis=-1, keepdims=True)
            mrep_s[g, hl] = jnp.broadcast_to(m, (tq, 128))
            acc_s[g, hl] = jnp.zeros((tq, 2 * HEAD_DIM), _F32)

    def close_values(g):
        for hl in range(head_group):
            h = g * head_group + hl
            acc = acc_s[g, hl]
            o_ref[0, :, h * HEAD_DIM:(h + 1) * HEAD_DIM] = (acc[:, :HEAD_DIM] / acc[:, HEAD_DIM:]).astype(o_ref.dtype)

    for g in range(n_groups + 1):
        if g < n_groups:
            start_group(g)

        def blk(kj, carry, g=g):
            if g < n_groups:
                logits_pass(kj, g)
            if g > 0:
                value_pass(kj, g - 1)
            return carry

        lax.fori_loop(0, qi + 1, blk, 0)
        if g < n_groups:
            close_logits(g)
        if g > 0:
            close_values(g - 1)


def _mixer_b(tab, ln_g, ln_b, big3, small3, tq):
    batch, seq, _ = big3.shape
    nkb = seq // tq
    head_group = 4
    full = lambda cb: pl.BlockSpec((1, seq, WIDTH_B), lambda b, i, cb=cb: (b, 0, cb),
                                   pipeline_mode=pl.Buffered(1))
    tile = lambda cb: pl.BlockSpec((1, tq, WIDTH_B), lambda b, i, cb=cb: (b, i, cb))
    small_cb = small3.shape[-1] // SMALL_W - 1
    return pl.pallas_call(
        functools.partial(_mixer_b_kernel, tq=tq, seq=seq),
        grid=(batch, nkb),
        in_specs=[
            pl.BlockSpec(memory_space=pltpu.SMEM),
            pl.BlockSpec((1, IDX_DIM), lambda b, i: (0, 0)),
            pl.BlockSpec((1, IDX_DIM), lambda b, i: (0, 0)),
            tile(0), full(1), full(2), tile(3),
            pl.BlockSpec((1, seq, SMALL_W), lambda b, i: (b, 0, small_cb), pipeline_mode=pl.Buffered(1)),
            pl.BlockSpec((1, tq, SMALL_W), lambda b, i: (b, i, small_cb)),
        ],
        out_specs=pl.BlockSpec((1, tq, WIDTH_B), lambda b, i: (b, i, 0)),
        out_shape=jax.ShapeDtypeStruct((batch, seq, WIDTH_B), _BF16),
        scratch_shapes=[
            pltpu.VMEM((2, seq, 2 * IDX_DIM), _BF16),
            pltpu.VMEM((nkb, HEADS_B, tq, tq), _F32),
            pltpu.VMEM((nkb, tq, tq), jnp.int32),
            pltpu.VMEM((nkb, tq, tq), jnp.int16),
            pltpu.VMEM((nkb, tq, tq), jnp.int16),
            pltpu.VMEM((nkb, tq, tq), _F32),
            pltpu.VMEM((SMALL_W, tq), _F32),
            pltpu.VMEM((HEADS_B // head_group, head_group, nkb, tq, tq), _F32),
            pltpu.VMEM((HEADS_B // head_group, head_group, tq, 128), _F32),
            pltpu.VMEM((HEADS_B // head_group, head_group, tq, 128), _F32),
            pltpu.VMEM((HEADS_B // head_group, head_group, tq, 2 * HEAD_DIM), _F32),
        ],
        compiler_params=pltpu.CompilerParams(
            dimension_semantics=("arbitrary", "arbitrary"), vmem_limit_bytes=V7X_VMEM_LIMIT),
        name="mixer_b",
    )(tab, ln_g, ln_b, big3, big3, big3, big3, small3, small3)


def _merge_kernel(x_ref, oa0_ref, oa1_ref, oa2_ref, l0_ref, l1_ref, l2_ref, ob_ref, ga_ref, gb_ref,
                  wpa_ref, wpb_ref, wout_ref, y_ref):
    def natural(ref, h, group):
        dil = DILATIONS[group]
        if not _tile_class_major(dil):
            return ref[0, h]
        per = MERGE_TILE // dil
        return jnp.concatenate([ref[0, h, pl.ds(slot, dil, stride=per), :] for slot in range(per)], axis=0)

    pb = jnp.dot(ob_ref[...], wpb_ref[...], preferred_element_type=_F32)
    heads = []
    for h in range(HEADS_PER_GROUP):
        l0, l1, l2 = natural(l0_ref, h, 0), natural(l1_ref, h, 1), natural(l2_ref, h, 2)
        m = jnp.maximum(jnp.maximum(l0, l1), l2)
        e0, e1, e2 = jnp.exp(l0 - m), jnp.exp(l1 - m), jnp.exp(l2 - m)
        heads.append((e0 * natural(oa0_ref, h, 0) + e1 * natural(oa1_ref, h, 1) + e2 * natural(oa2_ref, h, 2))
                     / (e0 + e1 + e2))
    o_a = jnp.concatenate(heads, axis=-1)
    pa = jnp.dot(o_a.astype(_BF16), wpa_ref[...], preferred_element_type=_F32)
    merged = jax.nn.sigmoid(ga_ref[...]) * pa + jax.nn.sigmoid(gb_ref[...]) * pb
    y_ref[...] = x_ref[...] + jnp.dot(merged.astype(_BF16), wout_ref[...], preferred_element_type=_F32)


def _merge(x2d, oa, lse, ob2d, fout, wpa, wpb, wout, tm):
    n, d = x2d.shape
    tiles_per_seq = oa[0].shape[2] // tm
    rows = lambda w, cb=0: pl.BlockSpec((tm, w), lambda i, cb=cb: (i, cb))
    heads = pl.BlockSpec((1, HEADS_PER_GROUP, tm, HEAD_DIM),
                         lambda i: (i // tiles_per_seq, 0, i % tiles_per_seq, 0))
    const = lambda shape: pl.BlockSpec(shape, lambda i: (0, 0), pipeline_mode=pl.Buffered(1))
    return pl.pallas_call(
        _merge_kernel,
        grid=(n // tm,),
        in_specs=[rows(d), heads, heads, heads, heads, heads, heads, rows(WIDTH_B), rows(d, 0), rows(d, 1),
                  const(wpa.shape), const(wpb.shape), const(wout.shape)],
        out_specs=rows(d),
        out_shape=jax.ShapeDtypeStruct((n, d), _F32),
        compiler_params=pltpu.CompilerParams(
            dimension_semantics=("arbitrary",), vmem_limit_bytes=V7X_VMEM_LIMIT),
        name="merge_out_proj",
    )(x2d, oa[0], oa[1], oa[2], lse[0], lse[1], lse[2], ob2d, fout, fout, wpa, wpb, wout)


def _mlp_kernel(x_ref, g_ref, gf_ref, wu_ref, wd_ref, y_ref, h_ref):
    j = pl.program_id(1)

    @pl.when(j == 0)
    def _():
        x = x_ref[...]
        ms = jnp.mean(x * x, axis=-1, keepdims=True)
        h_ref[...] = (x * lax.rsqrt(ms + NORM_EPS) * g_ref[...]).astype(_BF16)
        y_ref[...] = x

    u = jnp.maximum(jnp.dot(h_ref[...], wu_ref[...], preferred_element_type=_F32), 0.0)
    y_ref[...] += jnp.dot((u * u).astype(_BF16), wd_ref[...], preferred_element_type=_F32)

    @pl.when(j == pl.num_programs(1) - 1)
    def _():
        y = y_ref[...]
        ms = jnp.mean(y * y, axis=-1, keepdims=True)
        y_ref[...] = y * lax.rsqrt(ms + NORM_EPS) * gf_ref[...]


def _mlp(x2d, g, gf, wu, wd, tm, tf):
    n, d = x2d.shape
    dff = wu.shape[1]
    return pl.pallas_call(
        _mlp_kernel,
        grid=(n // tm, dff // tf),
        in_specs=[
            pl.BlockSpec((tm, d), lambda i, j: (i, 0)),
            pl.BlockSpec((1, d), lambda i, j: (0, 0)),
            pl.BlockSpec((1, d), lambda i, j: (0, 0)),
            pl.BlockSpec((d, tf), lambda i, j: (0, j)),
            pl.BlockSpec((tf, d), lambda i, j: (j, 0)),
        ],
        out_specs=pl.BlockSpec((tm, d), lambda i, j: (i, 0)),
        out_shape=jax.ShapeDtypeStruct((n, d), _F32),
        scratch_shapes=[pltpu.VMEM((tm, d), _BF16)],
        compiler_params=pltpu.CompilerParams(
            dimension_semantics=("arbitrary", "arbitrary"), vmem_limit_bytes=V7X_VMEM_LIMIT),
        name="mlp_final_norm",
    )(x2d, g, gf, wu, wd)


def _largest_tile(n, limit, step):
    t = (min(n, limit) // step) * step
    while n % t:
        t -= step
    return t


def kernel(x, norm_mix_g, w_in, idx_k_norm_g, idx_k_norm_b, rel_bias_table, w_proj_a, w_proj_b, w_out,
           norm_mlp_g, w_mlp_up, w_mlp_down, norm_final_g):
    batch, seq, d = x.shape
    assert seq == MAX_DISTANCE and seq % (DILATIONS[-1] * BAND) == 0
    assert w_in.shape[0] == 1, "single layer"
    n = batch * seq
    x2d = x.reshape(n, d)
    tab = rel_bias_table.reshape(-1)

    b0 = 3 * WIDTH_A
    k0 = b0 + 3 * WIDTH_B + IDX_W
    ga0 = k0 + IDX_DIM + IDX_HEADS
    wt = w_in[0].T

    h, small = _rmsnorm(x2d, norm_mix_g[0].reshape(1, d), wt, k0, _largest_tile(n, 1024, 8))
    big_b, w_up, w_down = _matmul(h, wt, _BF16, seq, GROUP_W, "proj_b_idx", row0=b0, cols=k0 - b0,
                                  round_too=(w_mlp_up[0], w_mlp_down[0]))
    gates = _matmul(h, wt, _F32, _largest_tile(seq, 1024, 256), _largest_tile(2 * d, 1024, 128),
                    "proj_gates", row0=ga0, cols=2 * d)

    oa, lse = [], []
    for group in range(N_GROUPS):
        a_g = _matmul_classmajor(h, wt, lambda j, group=group: (j * N_GROUPS + group) * GROUP_W, 3,
                                 DILATIONS[group], batch, seq, seq, GROUP_W, f"proj_a_g{group}")
        o_g, lse_g = _mixer_a_group(tab, a_g, group, batch, seq)
        oa.append(o_g)
        lse.append(lse_g)

    ob = _mixer_b(tab, idx_k_norm_g[0].reshape(1, IDX_DIM), idx_k_norm_b[0].reshape(1, IDX_DIM),
                  big_b.reshape(batch, seq, -1), small.reshape(batch, seq, -1), tq=256)

    x1 = _merge(x2d, oa, lse, ob.reshape(n, WIDTH_B), gates,
                w_proj_a[0].astype(_BF16), w_proj_b[0].astype(_BF16), w_out[0].astype(_BF16), tm=MERGE_TILE)
    y = _mlp(x1, norm_mlp_g[0].reshape(1, d), norm_final_g.reshape(1, d), w_up, w_down,
             tm=_largest_tile(n, 1024, 128), tf=_largest_tile(w_mlp_up.shape[-1], 512, 128))
    return y.reshape(batch, seq, d)
```

```python
import functools
import math

import jax
import jax.numpy as jnp
import numpy as np
from jax import lax
from jax.experimental import pallas as pl
from jax.experimental.pallas import tpu as pltpu

HEAD_DIM = 128
DILATIONS = (1, 4, 16)
N_GROUPS = 3
HEADS_PER_GROUP = 4
HEADS_A = N_GROUPS * HEADS_PER_GROUP
BAND = 128
HEADS_B = 8
N_HEADS = HEADS_A + HEADS_B
IDX_HEADS = 16
IDX_DIM = 64
TOPK = 256
N_BUCKETS = 32
MAX_DISTANCE = 2048
NORM_EPS = 1e-6
NEG_INF = -1e30
INT_MIN = -(2**31)

WIDTH_A = HEADS_A * HEAD_DIM
WIDTH_B = HEADS_B * HEAD_DIM
GROUP_W = HEADS_PER_GROUP * HEAD_DIM
IDX_W = IDX_HEADS * IDX_DIM
SMALL_W = 128
MERGE_TILE = 256

V7X_VMEM_LIMIT = 56 * 1024 * 1024

_F32 = jnp.float32
_BF16 = jnp.bfloat16


def _bucket_bounds():
    n = np.arange(MAX_DISTANCE, dtype=np.int32)
    max_exact = N_BUCKETS // 2
    nf = np.maximum(n, 1).astype(np.float32)
    large = max_exact + (
        np.log(nf / np.float32(max_exact)) / np.float32(math.log(MAX_DISTANCE / max_exact))
        * np.float32(N_BUCKETS - max_exact)
    ).astype(np.int32)
    large = np.minimum(large, N_BUCKETS - 1)
    bucket = np.where(n < max_exact, n, large)
    assert np.all(np.diff(bucket) >= 0)
    bounds = [int(np.argmax(bucket >= b)) if np.any(bucket >= b) else MAX_DISTANCE for b in range(N_BUCKETS)]
    return tuple(bounds)


_BOUNDS = _bucket_bounds()


def _dot_nt(a, b):
    return lax.dot_general(a, b, (((1,), (1,)), ((), ())), preferred_element_type=_F32)


def _bias_from_distance(tab_ref, dist, head):
    val = jnp.full(dist.shape, tab_ref[head], _F32)
    for bkt in range(1, N_BUCKETS):
        val = jnp.where(dist >= _BOUNDS[bkt], tab_ref[bkt * N_HEADS + head], val)
    return val


def _rmsnorm_kernel(x_ref, g_ref, w_ref, h_ref, small_ref):
    x = x_ref[...]
    ms = jnp.mean(x * x, axis=-1, keepdims=True)
    h = (x * lax.rsqrt(ms + NORM_EPS) * g_ref[...]).astype(h_ref.dtype)
    h_ref[...] = h
    small_ref[...] = _dot_nt(h, w_ref[...].astype(h_ref.dtype))


def _rmsnorm(x2d, g, wt, small_row0, tm):
    n, d = x2d.shape
    return pl.pallas_call(
        _rmsnorm_kernel,
        grid=(n // tm,),
        in_specs=[pl.BlockSpec((tm, d), lambda i: (i, 0)), pl.BlockSpec((1, d), lambda i: (0, 0)),
                  pl.BlockSpec((pl.Element(SMALL_W), pl.Element(d)), lambda i: (small_row0, 0))],
        out_specs=[pl.BlockSpec((tm, d), lambda i: (i, 0)), pl.BlockSpec((tm, SMALL_W), lambda i: (i, 0))],
        out_shape=[jax.ShapeDtypeStruct((n, d), _BF16), jax.ShapeDtypeStruct((n, SMALL_W), _F32)],
        compiler_params=pltpu.CompilerParams(
            dimension_semantics=("arbitrary",), vmem_limit_bytes=V7X_VMEM_LIMIT),
        name="rmsnorm_mix",
    )(x2d, g, wt)


def _cast_weight_tile(w_ref, wbf_ref):
    @pl.when(pl.program_id(1) == 0)
    def _():
        wbf_ref[...] = w_ref[...].astype(wbf_ref.dtype)


def _matmul_kernel(h_ref, w_ref, o_ref, wbf_ref):
    _cast_weight_tile(w_ref, wbf_ref)
    o_ref[...] = _dot_nt(h_ref[...], wbf_ref[...]).astype(o_ref.dtype)


def _weight_rows_spec(tn, d, row_start):
    return pl.BlockSpec((pl.Element(tn), pl.Element(d)), lambda j, i: (pl.multiple_of(row_start(j), 8), 0))


def _matmul_and_round_kernel(h_ref, w_ref, wu_ref, wd_ref, o_ref, wu_bf_ref, wd_bf_ref, wbf_ref):
    wu_bf_ref[...] = wu_ref[...].astype(wu_bf_ref.dtype)
    wd_bf_ref[...] = wd_ref[...].astype(wd_bf_ref.dtype)
    _matmul_kernel(h_ref, w_ref, o_ref, wbf_ref)


def _matmul(h, wt, out_dtype, tm, tn, name, row0, cols, round_too=None):
    n, d = h.shape
    assert cols % tn == 0
    n_i = n // tm
    steps = (cols // tn) * n_i
    in_specs = [pl.BlockSpec((tm, d), lambda j, i: (i, 0)), _weight_rows_spec(tn, d, lambda j: row0 + j * tn)]
    out_specs = [pl.BlockSpec((tm, tn), lambda j, i: (i, j))]
    out_shape = [jax.ShapeDtypeStruct((n, cols), out_dtype)]
    operands = [h, wt]
    kernel_fn = _matmul_kernel
    if round_too is not None:
        w_up, w_down = round_too
        ff = w_up.shape[1] // steps
        assert w_up.shape[1] % steps == 0 and ff % 128 == 0
        slices = [pl.BlockSpec((w_up.shape[0], ff), lambda j, i: (0, j * n_i + i)),
                  pl.BlockSpec((ff, w_down.shape[1]), lambda j, i: (j * n_i + i, 0))]
        in_specs += slices
        out_specs += slices
        out_shape += [jax.ShapeDtypeStruct(w_up.shape, _BF16), jax.ShapeDtypeStruct(w_down.shape, _BF16)]
        operands += [w_up, w_down]
        kernel_fn = _matmul_and_round_kernel
    out = pl.pallas_call(
        kernel_fn,
        grid=(cols // tn, n_i),
        in_specs=in_specs,
        out_specs=out_specs,
        out_shape=out_shape,
        scratch_shapes=[pltpu.VMEM((tn, d), _BF16)],
        compiler_params=pltpu.CompilerParams(
            dimension_semantics=("arbitrary", "arbitrary"), vmem_limit_bytes=V7X_VMEM_LIMIT),
        name=name,
    )(*operands)
    return out[0] if round_too is None else out


def _matmul_classmajor_kernel(h_ref, w_ref, o_ref, acc_ref, tmp_ref, wbf_ref, *, dil, rows):
    _cast_weight_tile(w_ref, wbf_ref)
    res = _dot_nt(h_ref[...], wbf_ref[...])
    if dil == 1:
        o_ref[0, 0] = res.astype(o_ref.dtype)
        return
    for c in range(acc_ref.shape[0]):
        acc_ref[c] = res[:, c * 128:(c + 1) * 128]
    if not _tile_class_major(dil):
        for c in range(acc_ref.shape[0]):
            for r in range(dil):
                o_ref[0, r, :, c * 128:(c + 1) * 128] = (
                    acc_ref[c, pl.ds(r, rows, stride=dil), :].astype(o_ref.dtype))
        return
    s1 = 4
    s2 = dil // s1
    step_rows = rows * s2
    for c in range(acc_ref.shape[0]):
        for r1 in range(s1):
            tmp_ref[c, r1 * step_rows:(r1 + 1) * step_rows, :] = acc_ref[c, pl.ds(r1, step_rows, stride=s1), :]
        for r1 in range(s1):
            for r2 in range(s2):
                o_ref[0, r1 + s1 * r2, :, c * 128:(c + 1) * 128] = (
                    tmp_ref[c, pl.ds(r1 * step_rows + r2, rows, stride=s2), :].astype(o_ref.dtype))


def _matmul_classmajor(h, wt, row_start, n_blocks, dil, batch, seq, tm, tn, name):
    n, d = h.shape
    cols = n_blocks * tn
    tiles_per_seq = seq // tm
    rows = tm // dil
    return pl.pallas_call(
        functools.partial(_matmul_classmajor_kernel, dil=dil, rows=rows),
        grid=(n_blocks, n // tm),
        in_specs=[pl.BlockSpec((tm, d), lambda j, i: (i, 0)), _weight_rows_spec(tn, d, row_start)],
        out_specs=pl.BlockSpec((1, dil, rows, tn),
                               lambda j, i: (i // tiles_per_seq, 0, i % tiles_per_seq, j)),
        out_shape=jax.ShapeDtypeStruct((batch, dil, seq // dil, cols), _BF16),
        scratch_shapes=[pltpu.VMEM((tn // 128, tm, 128), _F32),
                        pltpu.VMEM((tn // 128, tm if _tile_class_major(dil) else 8, 128), _F32),
                        pltpu.VMEM((tn, d), _BF16)],
        compiler_params=pltpu.CompilerParams(
            dimension_semantics=("arbitrary", "arbitrary"), vmem_limit_bytes=V7X_VMEM_LIMIT),
        name=name,
    )(h, wt)


def _tile_class_major(dil):
    return dil % 8 == 0


def _mixer_a_kernel(tab_ref, *refs, seq):
    qkv_refs, o_ref, (bias_ref, o_s, lse_s) = refs[:3 * N_GROUPS], refs[3 * N_GROUPS], refs[3 * N_GROUPS + 1:]
    h = pl.program_id(1)
    scale = HEAD_DIM ** -0.5

    @pl.when((pl.program_id(0) == 0) & (h == 0))
    def _build_bias():
        row = lax.broadcasted_iota(jnp.int32, (BAND, BAND), 0)
        col = lax.broadcasted_iota(jnp.int32, (BAND, BAND), 1)
        for side in range(2):
            j = row - col + BAND * (1 - side)
            valid = jnp.where(j >= 0, j, BAND + 1) <= BAND
            for head in range(HEADS_A):
                val = _bias_from_distance(tab_ref, j * DILATIONS[head // HEADS_PER_GROUP], head)
                bias_ref[head, :, side * BAND:(side + 1) * BAND] = jnp.where(valid, val, NEG_INF)

    def key_rows(i):
        return pl.ds(0, BAND) if i == 0 else pl.ds((i - 1) * BAND, 2 * BAND)

    def logits(g, r, i):
        q_ref, k_ref = qkv_refs[3 * g], qkv_refs[3 * g + 1]
        bias = bias_ref[g * HEADS_PER_GROUP + h]
        if i == 0:
            bias = bias[:, BAND:]
        return _dot_nt(q_ref[0, r, pl.ds(i * BAND, BAND), :], k_ref[0, r, key_rows(i), :]) * scale + bias

    def finish(g, r, i, s):
        dil = DILATIONS[g]
        m = jnp.max(s, axis=-1, keepdims=True)
        p = jnp.exp(s - m).astype(_BF16)
        v = qkv_refs[3 * g + 2][0, r, key_rows(i), :]
        res = jnp.dot(p, jnp.concatenate([v, jnp.ones_like(v)], axis=-1), preferred_element_type=_F32)
        den = res[:, HEAD_DIM:]
        rows = pl.ds(i * BAND, BAND) if dil == 1 else pl.ds(r + i * BAND * dil, BAND, stride=dil)
        o_s[g, rows, :] = res[:, :HEAD_DIM] / den
        lse_s[g, rows, :] = m + jnp.log(den)

    units = [(g, r, i) for g, dil in enumerate(DILATIONS) for r in range(dil) for i in range(seq // dil // BAND)]
    in_flight = 4
    pending = None
    for start in range(0, len(units), in_flight):
        cur = [(g, r, i, logits(g, r, i)) for g, r, i in units[start:start + in_flight]]
        if pending is not None:
            for unit in pending:
                finish(*unit)
        pending = cur
    for unit in pending:
        finish(*unit)

    def mix(c, carry):
        rows = pl.ds(pl.multiple_of(c * MERGE_TILE, MERGE_TILE), MERGE_TILE)
        l0, l1, l2 = lse_s[0, rows, :], lse_s[1, rows, :], lse_s[2, rows, :]
        m = jnp.maximum(jnp.maximum(l0, l1), l2)
        e0, e1, e2 = jnp.exp(l0 - m), jnp.exp(l1 - m), jnp.exp(l2 - m)
        o = (e0 * o_s[0, rows, :] + e1 * o_s[1, rows, :] + e2 * o_s[2, rows, :]) / (e0 + e1 + e2)
        o_ref[0, rows, :] = o.astype(o_ref.dtype)
        return carry

    lax.fori_loop(0, seq // MERGE_TILE, mix, 0)


def _mixer_a(tab, a_groups, batch, seq):
    in_specs = [pl.BlockSpec(memory_space=pltpu.SMEM)]
    operands = [tab]
    for g, dil in enumerate(DILATIONS):
        for which in range(3):
            in_specs.append(pl.BlockSpec((1, dil, seq // dil, HEAD_DIM),
                                         lambda b, h, which=which: (b, 0, 0, which * HEADS_PER_GROUP + h)))
            operands.append(a_groups[g])
    return pl.pallas_call(
        functools.partial(_mixer_a_kernel, seq=seq),
        grid=(batch, HEADS_PER_GROUP),
        in_specs=in_specs,
        out_specs=pl.BlockSpec((1, seq, HEAD_DIM), lambda b, h: (b, 0, h)),
        out_shape=jax.ShapeDtypeStruct((batch, seq, GROUP_W), _BF16),
        scratch_shapes=[pltpu.VMEM((HEADS_A, BAND, 2 * BAND), _F32),
                        pltpu.VMEM((N_GROUPS, seq, HEAD_DIM), _F32),
                        pltpu.VMEM((N_GROUPS, seq, HEAD_DIM), _F32)],
        compiler_params=pltpu.CompilerParams(
            dimension_semantics=("arbitrary", "arbitrary"), vmem_limit_bytes=V7X_VMEM_LIMIT),
        name="mixer_a",
    )(*operands)


def _mixer_b_kernel(tab_ref, lng_ref, lnb_ref, q_ref, k_ref, v_ref, qidx_ref, smallk_ref, smallq_ref,
                    o_ref, kidx_s, bias_s, key_s, hi_s, lo_s, selb_s, wt_s, s_s, mx_s, mrep_s, acc_s,
                    *, tq, seq):
    b = pl.program_id(0)
    qi = pl.program_id(1)
    nkb = seq // tq
    lane_tiles = tq // 128
    scale = HEAD_DIM ** -0.5
    inv_scale = HEAD_DIM ** 0.5
    neg_scaled = NEG_INF * inv_scale
    row = lax.broadcasted_iota(jnp.int32, (tq, tq), 0)
    col = lax.broadcasted_iota(jnp.int32, (tq, tq), 1)

    @pl.when((b == 0) & (qi == 0))
    def _build_bias():
        def per_delta(dl, carry):
            dist = row - col + dl * tq
            for h in range(HEADS_B):
                bias_s[dl, h] = _bias_from_distance(tab_ref, dist, HEADS_A + h) * inv_scale
            return carry

        lax.fori_loop(0, nkb, per_delta, 0)

    @pl.when(qi == 0)
    def _layer_norm_keys():
        def chunk(c, carry):
            r = pl.multiple_of(c * tq, tq)
            kx = smallk_ref[0, pl.ds(r, tq), :][:, :IDX_DIM]
            mu = jnp.mean(kx, axis=-1, keepdims=True)
            xc = kx - mu
            var = jnp.mean(xc * xc, axis=-1, keepdims=True)
            y = xc * lax.rsqrt(var + NORM_EPS) * lng_ref[...] + lnb_ref[...]
            z = jnp.zeros_like(y)
            kidx_s[0, pl.ds(r, tq), :] = jnp.concatenate([y, z], axis=-1).astype(_BF16)
            kidx_s[1, pl.ds(r, tq), :] = jnp.concatenate([z, y], axis=-1).astype(_BF16)
            return carry

        lax.fori_loop(0, nkb, chunk, 0)

    wt_s[...] = (smallq_ref[0] * (IDX_HEADS ** -0.5 * IDX_DIM ** -0.5)).T

    def score_blk(kj, carry):
        r = pl.multiple_of(kj * tq, tq)
        acc = jnp.zeros((tq, tq), _F32)
        for h in range(IDX_HEADS):
            pair = qidx_ref[0][:, (h // 2) * 128:(h // 2 + 1) * 128]
            x = _dot_nt(kidx_s[h % 2, pl.ds(r, tq), :], pair)
            acc = acc + wt_s[IDX_DIM + h:IDX_DIM + h + 1, :] * jnp.maximum(x, 0.0)
        bits = lax.bitcast_convert_type(acc, jnp.int32)
        key = bits ^ ((bits >> 31) & 0x7FFFFFFF)
        causal = (kj * tq + row) <= (qi * tq + col)
        key = jnp.where(causal, key, INT_MIN)
        key_s[kj] = key
        hi_s[kj] = (key >> 16).astype(jnp.int16)
        lo_s[kj] = ((key & 0xFFFF) - 32768).astype(jnp.int16)
        return carry

    lax.fori_loop(0, qi + 1, score_blk, 0)

    def count_ge(cand):
        def blk(kj, cnt):
            kk = key_s[kj].reshape(tq // 8, 8, tq)
            return cnt + jnp.sum(jnp.where(kk >= cand[None], 1.0, 0.0), axis=0)

        cnt = lax.fori_loop(0, qi + 1, blk, jnp.zeros((8, tq), _F32))
        return jnp.broadcast_to(jnp.sum(cnt, axis=0, keepdims=True), (8, tq))

    one16, zero16, min16 = jnp.ones((), jnp.int16), jnp.zeros((), jnp.int16), jnp.full((), -32768, jnp.int16)

    def packed(cand):
        return jnp.broadcast_to(cand[:1], (16, tq)).astype(jnp.int16)

    def count16(src, cand, strict=False):
        c16 = packed(cand)[None]

        def blk(kj, cnt):
            kk = src[kj].reshape(tq // 16, 16, tq)
            hit = jnp.where((kk > c16) if strict else (kk >= c16), one16, zero16)
            parts = [hit[t] for t in range(tq // 16)]
            while len(parts) > 1:
                parts = [parts[t] + parts[t + 1] for t in range(0, len(parts), 2)]
            return cnt + parts[0]

        cnt = lax.fori_loop(0, qi + 1, blk, jnp.zeros((16, tq), jnp.int16))
        return jnp.broadcast_to(jnp.sum(cnt.astype(_F32), axis=0, keepdims=True), (8, tq))

    def search16(src, target, cnt_floor):
        zero = jnp.zeros((8, tq), jnp.int32)
        cnt0 = count16(src, zero)
        ok0 = cnt0 >= target
        init = (jnp.where(ok0, zero, -32768), jnp.where(ok0, cnt0, cnt_floor))

        def bit_body(bi, carry):
            prefix, cnt_prefix = carry
            cand = prefix | lax.shift_left(jnp.int32(1), 14 - bi)
            cnt = count16(src, cand)
            ok = cnt >= target
            return jnp.where(ok, cand, prefix), jnp.where(ok, cnt, cnt_prefix)

        return lax.fori_loop(0, 15, bit_body, init)

    def kth_largest_key():
        total = jnp.zeros((8, tq), _F32) + ((qi + 1) * tq).astype(_F32)
        tau_hi, cnt_hi = search16(hi_s, float(TOPK), total)
        above_hi = count16(hi_s, tau_hi, strict=True)
        tau_hi16 = packed(tau_hi)[None]

        def mask_lo(kj, carry):
            hi = hi_s[kj].reshape(tq // 16, 16, tq)
            lo = lo_s[kj].reshape(tq // 16, 16, tq)
            lo_s[kj] = jnp.where(hi == tau_hi16, lo, min16).reshape(tq, tq)
            return carry

        lax.fori_loop(0, qi + 1, mask_lo, 0)
        tau_lo, cnt_lo = search16(lo_s, TOPK - above_hi, cnt_hi - above_hi)
        return tau_hi * 65536 + (tau_lo + 32768), above_hi + cnt_lo

    def every_key():
        return jnp.full((8, tq), INT_MIN, jnp.int32), jnp.zeros((8, tq), _F32)

    tau, cnt_tau = lax.cond((qi + 1) * tq > TOPK, kth_largest_key, every_key)
    thr = jnp.maximum(tau, INT_MIN + 1)
    has_ties = jnp.max(jnp.where(tau > INT_MIN, cnt_tau, 0.0)) > TOPK

    @pl.when(jnp.logical_not(has_ties))
    def _select_by_threshold():
        def write_sel(kj, carry):
            kk = key_s[kj].reshape(tq // 8, 8, tq)
            sel_t = jnp.where(kk >= thr[None], 0.0, neg_scaled).reshape(tq, tq)
            selb_s[kj] = sel_t.T
            return carry

        lax.fori_loop(0, qi + 1, write_sel, 0)

    @pl.when(has_ties)
    def _select_breaking_ties():
        need = (TOPK - count_ge(thr + 1))[:1]
        earlier = jnp.where(row > col, 1.0, 0.0).astype(_BF16)

        def write_sel(kj, seen):
            kk = key_s[kj].reshape(tq // 8, 8, tq)
            above = jnp.where(kk > thr[None], 1.0, 0.0).reshape(tq, tq)
            equal = jnp.where(kk == thr[None], 1.0, 0.0).reshape(tq, tq)
            rank = seen + jnp.dot(earlier, equal.astype(_BF16), preferred_element_type=_F32)
            take = above + equal * jnp.where(rank < need, 1.0, 0.0)
            selb_s[kj] = jnp.where(take > 0.0, 0.0, neg_scaled).T
            return seen + jnp.sum(equal, axis=0, keepdims=True)

        lax.fori_loop(0, qi + 1, write_sel, jnp.zeros((1, tq), _F32))

    n_groups, head_group = s_s.shape[:2]
    exp2_coeff = scale * math.log2(math.e)

    def logits_pass(kj, g):
        r = pl.multiple_of(kj * tq, tq)
        for hl in range(head_group):
            h = g * head_group + hl
            cs = slice(h * HEAD_DIM, (h + 1) * HEAD_DIM)
            s = _dot_nt(q_ref[0][:, cs], k_ref[0, pl.ds(r, tq), cs]) + bias_s[qi - kj, h] + selb_s[kj]
            s_s[g, hl, kj] = s
            part = s[:, :128]
            for c in range(1, lane_tiles):
                part = jnp.maximum(part, s[:, c * 128:(c + 1) * 128])
            mx_s[g, hl] = jnp.maximum(mx_s[g, hl], part)

    def value_pass(kj, g):
        r = pl.multiple_of(kj * tq, tq)
        for hl in range(head_group):
            h = g * head_group + hl
            m = mrep_s[g, hl]
            p = jnp.exp2((s_s[g, hl, kj] - jnp.concatenate([m] * lane_tiles, axis=-1)) * exp2_coeff)
            v = v_ref[0, pl.ds(r, tq), h * HEAD_DIM:(h + 1) * HEAD_DIM]
            acc_s[g, hl] = acc_s[g, hl] + jnp.dot(
                p.astype(_BF16), jnp.concatenate([v, jnp.ones_like(v)], axis=-1), preferred_element_type=_F32)

    def start_group(g):
        for hl in range(head_group):
            mx_s[g, hl] = jnp.full((tq, 128), -3e38, _F32)

    def close_logits(g):
        for hl in range(head_group):
            m = jnp.max(mx_s[g, hl], axis=-1, keepdims=True)
            mrep_s[g, hl] = jnp.broadcast_to(m, (tq, 128))
            acc_s[g, hl] = jnp.zeros((tq, 2 * HEAD_DIM), _F32)

    def close_values(g):
        for hl in range(head_group):
            h = g * head_group + hl
            acc = acc_s[g, hl]
            o_ref[0, :, h * HEAD_DIM:(h + 1) * HEAD_DIM] = (acc[:, :HEAD_DIM] / acc[:, HEAD_DIM:]).astype(o_ref.dtype)

    for g in range(n_groups + 1):
        if g < n_groups:
            start_group(g)

        def blk(kj, carry, g=g):
            if g < n_groups:
                logits_pass(kj, g)
            if g > 0:
                value_pass(kj, g - 1)
            return carry

        lax.fori_loop(0, qi + 1, blk, 0)
        if g < n_groups:
            close_logits(g)
        if g > 0:
            close_values(g - 1)


def _mixer_b(tab, ln_g, ln_b, big3, small3, tq):
    batch, seq, _ = big3.shape
    nkb = seq // tq
    head_group = 4
    full = lambda cb: pl.BlockSpec((1, seq, WIDTH_B), lambda b, i, cb=cb: (b, 0, cb),
                                   pipeline_mode=pl.Buffered(1))
    tile = lambda cb: pl.BlockSpec((1, tq, WIDTH_B), lambda b, i, cb=cb: (b, i, cb))
    small_cb = small3.shape[-1] // SMALL_W - 1
    return pl.pallas_call(
        functools.partial(_mixer_b_kernel, tq=tq, seq=seq),
        grid=(batch, nkb),
        in_specs=[
            pl.BlockSpec(memory_space=pltpu.SMEM),
            pl.BlockSpec((1, IDX_DIM), lambda b, i: (0, 0)),
            pl.BlockSpec((1, IDX_DIM), lambda b, i: (0, 0)),
            tile(0), full(1), full(2), tile(3),
            pl.BlockSpec((1, seq, SMALL_W), lambda b, i: (b, 0, small_cb), pipeline_mode=pl.Buffered(1)),
            pl.BlockSpec((1, tq, SMALL_W), lambda b, i: (b, i, small_cb)),
        ],
        out_specs=pl.BlockSpec((1, tq, WIDTH_B), lambda b, i: (b, i, 0)),
        out_shape=jax.ShapeDtypeStruct((batch, seq, WIDTH_B), _BF16),
        scratch_shapes=[
            pltpu.VMEM((2, seq, 2 * IDX_DIM), _BF16),
            pltpu.VMEM((nkb, HEADS_B, tq, tq), _F32),
            pltpu.VMEM((nkb, tq, tq), jnp.int32),
            pltpu.VMEM((nkb, tq, tq), jnp.int16),
            pltpu.VMEM((nkb, tq, tq), jnp.int16),
            pltpu.VMEM((nkb, tq, tq), _F32),
            pltpu.VMEM((SMALL_W, tq), _F32),
            pltpu.VMEM((HEADS_B // head_group, head_group, nkb, tq, tq), _F32),
            pltpu.VMEM((HEADS_B // head_group, head_group, tq, 128), _F32),
            pltpu.VMEM((HEADS_B // head_group, head_group, tq, 128), _F32),
            pltpu.VMEM((HEADS_B // head_group, head_group, tq, 2 * HEAD_DIM), _F32),
        ],
        compiler_params=pltpu.CompilerParams(
            dimension_semantics=("arbitrary", "arbitrary"), vmem_limit_bytes=V7X_VMEM_LIMIT),
        name="mixer_b",
    )(tab, ln_g, ln_b, big3, big3, big3, big3, small3, small3)


def _merge_kernel(x_ref, oa_ref, ob_ref, ga_ref, gb_ref, wpa_ref, wpb_ref, wout_ref, y_ref):
    pa = jnp.dot(oa_ref[...], wpa_ref[...], preferred_element_type=_F32)
    pb = jnp.dot(ob_ref[...], wpb_ref[...], preferred_element_type=_F32)
    merged = jax.nn.sigmoid(ga_ref[...]) * pa + jax.nn.sigmoid(gb_ref[...]) * pb
    y_ref[...] = x_ref[...] + jnp.dot(merged.astype(_BF16), wout_ref[...], preferred_element_type=_F32)


def _merge(x2d, oa2d, ob2d, gates, wpa, wpb, wout, tm):
    n, d = x2d.shape
    rows = lambda w, cb=0: pl.BlockSpec((tm, w), lambda i, cb=cb: (i, cb))
    const = lambda shape: pl.BlockSpec(shape, lambda i: (0, 0), pipeline_mode=pl.Buffered(1))
    return pl.pallas_call(
        _merge_kernel,
        grid=(n // tm,),
        in_specs=[rows(d), rows(GROUP_W), rows(WIDTH_B), rows(d, 0), rows(d, 1),
                  const(wpa.shape), const(wpb.shape), const(wout.shape)],
        out_specs=rows(d),
        out_shape=jax.ShapeDtypeStruct((n, d), _F32),
        compiler_params=pltpu.CompilerParams(
            dimension_semantics=("arbitrary",), vmem_limit_bytes=V7X_VMEM_LIMIT),
        name="merge_out_proj",
    )(x2d, oa2d, ob2d, gates, gates, wpa, wpb, wout)


def _mlp_kernel(x_ref, g_ref, gf_ref, wu_ref, wd_ref, y_ref, h_ref):
    j = pl.program_id(1)

    @pl.when(j == 0)
    def _():
        x = x_ref[...]
        ms = jnp.mean(x * x, axis=-1, keepdims=True)
        h_ref[...] = (x * lax.rsqrt(ms + NORM_EPS) * g_ref[...]).astype(_BF16)
        y_ref[...] = x

    u = jnp.maximum(jnp.dot(h_ref[...], wu_ref[...], preferred_element_type=_F32), 0.0)
    y_ref[...] += jnp.dot((u * u).astype(_BF16), wd_ref[...], preferred_element_type=_F32)

    @pl.when(j == pl.num_programs(1) - 1)
    def _():
        y = y_ref[...]
        ms = jnp.mean(y * y, axis=-1, keepdims=True)
        y_ref[...] = y * lax.rsqrt(ms + NORM_EPS) * gf_ref[...]


def _mlp(x2d, g, gf, wu, wd, tm, tf):
    n, d = x2d.shape
    dff = wu.shape[1]
    return pl.pallas_call(
        _mlp_kernel,
        grid=(n // tm, dff // tf),
        in_specs=[
            pl.BlockSpec((tm, d), lambda i, j: (i, 0)),
            pl.BlockSpec((1, d), lambda i, j: (0, 0)),
            pl.BlockSpec((1, d), lambda i, j: (0, 0)),
            pl.BlockSpec((d, tf), lambda i, j: (0, j)),
            pl.BlockSpec((tf, d), lambda i, j: (j, 0)),
        ],
        out_specs=pl.BlockSpec((tm, d), lambda i, j: (i, 0)),
        out_shape=jax.ShapeDtypeStruct((n, d), _F32),
        scratch_shapes=[pltpu.VMEM((tm, d), _BF16)],
        compiler_params=pltpu.CompilerParams(
            dimension_semantics=("arbitrary", "arbitrary"), vmem_limit_bytes=V7X_VMEM_LIMIT),
        name="mlp_final_norm",
    )(x2d, g, gf, wu, wd)


def _largest_tile(n, limit, step):
    t = (min(n, limit) // step) * step
    while n % t:
        t -= step
    return t


def kernel(x, norm_mix_g, w_in, idx_k_norm_g, idx_k_norm_b, rel_bias_table, w_proj_a, w_proj_b, w_out,
           norm_mlp_g, w_mlp_up, w_mlp_down, norm_final_g):
    batch, seq, d = x.shape
    assert seq == MAX_DISTANCE and seq % (DILATIONS[-1] * BAND) == 0
    assert w_in.shape[0] == 1, "single layer"
    n = batch * seq
    x2d = x.reshape(n, d)
    tab = rel_bias_table.reshape(-1)

    b0 = 3 * WIDTH_A
    k0 = b0 + 3 * WIDTH_B + IDX_W
    ga0 = k0 + IDX_DIM + IDX_HEADS
    wt = w_in[0].T

    h, small = _rmsnorm(x2d, norm_mix_g[0].reshape(1, d), wt, k0, _largest_tile(n, 1024, 8))
    big_b, w_up, w_down = _matmul(h, wt, _BF16, seq, GROUP_W, "proj_b_idx", row0=b0, cols=k0 - b0,
                                  round_too=(w_mlp_up[0], w_mlp_down[0]))
    gates = _matmul(h, wt, _F32, _largest_tile(seq, 1024, 256), _largest_tile(2 * d, 1024, 128),
                    "proj_gates", row0=ga0, cols=2 * d)

    a_groups = [_matmul_classmajor(h, wt, lambda j, group=group: (j * N_GROUPS + group) * GROUP_W, 3,
                                   DILATIONS[group], batch, seq, seq, GROUP_W, f"proj_a_g{group}")
                for group in range(N_GROUPS)]
    oa = _mixer_a(tab, a_groups, batch, seq)

    ob = _mixer_b(tab, idx_k_norm_g[0].reshape(1, IDX_DIM), idx_k_norm_b[0].reshape(1, IDX_DIM),
                  big_b.reshape(batch, seq, -1), small.reshape(batch, seq, -1), tq=256)

    x1 = _merge(x2d, oa.reshape(n, GROUP_W), ob.reshape(n, WIDTH_B), gates,
                w_proj_a[0].astype(_BF16), w_proj_b[0].astype(_BF16), w_out[0].astype(_BF16), tm=MERGE_TILE)
    y = _mlp(x1, norm_mlp_g[0].reshape(1, d), norm_final_g.reshape(1, d), w_up, w_down,
             tm=_largest_tile(n, 1024, 128), tf=_largest_tile(w_mlp_up.shape[-1], 512, 128))
    return y.reshape(batch, seq, d)
```

```python
import functools
import math

import jax
import jax.numpy as jnp
import numpy as np
from jax import lax
from jax.experimental import pallas as pl
from jax.experimental.pallas import tpu as pltpu

HEAD_DIM = 128
DILATIONS = (1, 4, 16)
N_GROUPS = 3
HEADS_PER_GROUP = 4
HEADS_A = N_GROUPS * HEADS_PER_GROUP
BAND = 128
HEADS_B = 8
N_HEADS = HEADS_A + HEADS_B
IDX_HEADS = 16
IDX_DIM = 64
TOPK = 256
N_BUCKETS = 32
MAX_DISTANCE = 2048
NORM_EPS = 1e-6
NEG_INF = -1e30
INT_MIN = -(2**31)

WIDTH_A = HEADS_A * HEAD_DIM
WIDTH_B = HEADS_B * HEAD_DIM
GROUP_W = HEADS_PER_GROUP * HEAD_DIM
IDX_W = IDX_HEADS * IDX_DIM
SMALL_W = 128
MERGE_TILE = 256

V7X_VMEM_BYTES = 64 * 1024 * 1024
V7X_VMEM_LIMIT = V7X_VMEM_BYTES * 7 // 8

_F32 = jnp.float32
_BF16 = jnp.bfloat16


def _bucket_bounds():
    n = np.arange(MAX_DISTANCE, dtype=np.int32)
    max_exact = N_BUCKETS // 2
    nf = np.maximum(n, 1).astype(np.float32)
    large = max_exact + (
        np.log(nf / np.float32(max_exact)) / np.float32(math.log(MAX_DISTANCE / max_exact))
        * np.float32(N_BUCKETS - max_exact)
    ).astype(np.int32)
    large = np.minimum(large, N_BUCKETS - 1)
    bucket = np.where(n < max_exact, n, large)
    assert np.all(np.diff(bucket) >= 0)
    bounds = [int(np.argmax(bucket >= b)) if np.any(bucket >= b) else MAX_DISTANCE for b in range(N_BUCKETS)]
    return tuple(bounds)


_BOUNDS = _bucket_bounds()


def _dot_nt(a, b):
    return lax.dot_general(a, b, (((1,), (1,)), ((), ())), preferred_element_type=_F32)


def _bias_from_distance(tab_ref, dist, head):
    val = jnp.full(dist.shape, tab_ref[head], _F32)
    for bkt in range(1, N_BUCKETS):
        val = jnp.where(dist >= _BOUNDS[bkt], tab_ref[bkt * N_HEADS + head], val)
    return val


def _rmsnorm_kernel(x_ref, g_ref, w_ref, *refs):
    n_round = (len(refs) - 2) // 2
    to_round, (h_ref, small_ref), rounded = refs[:n_round], refs[n_round:n_round + 2], refs[n_round + 2:]
    for src, dst in zip(to_round, rounded):
        dst[...] = src[...].astype(dst.dtype)
    x = x_ref[...]
    ms = jnp.mean(x * x, axis=-1, keepdims=True)
    h = (x * lax.rsqrt(ms + NORM_EPS) * g_ref[...]).astype(h_ref.dtype)
    h_ref[...] = h
    small_ref[...] = _dot_nt(h, w_ref[...].astype(h_ref.dtype))


def _rmsnorm(x2d, g, wt, small_row0, tm, round_too):
    n, d = x2d.shape
    steps = n // tm
    slices = []
    for w in round_too:
        rows = w.shape[0] // steps
        assert w.shape[0] % steps == 0 and rows % 16 == 0
        slices.append(pl.BlockSpec((rows, w.shape[1]), lambda i: (i, 0)))
    return pl.pallas_call(
        _rmsnorm_kernel,
        grid=(steps,),
        in_specs=[pl.BlockSpec((tm, d), lambda i: (i, 0)), pl.BlockSpec((1, d), lambda i: (0, 0)),
                  pl.BlockSpec((pl.Element(SMALL_W), pl.Element(d)), lambda i: (small_row0, 0))] + slices,
        out_specs=[pl.BlockSpec((tm, d), lambda i: (i, 0)), pl.BlockSpec((tm, SMALL_W), lambda i: (i, 0))] + slices,
        out_shape=[jax.ShapeDtypeStruct((n, d), _BF16), jax.ShapeDtypeStruct((n, SMALL_W), _F32)]
        + [jax.ShapeDtypeStruct(w.shape, _BF16) for w in round_too],
        compiler_params=pltpu.CompilerParams(
            dimension_semantics=("arbitrary",), vmem_limit_bytes=V7X_VMEM_LIMIT),
        name="rmsnorm_mix",
    )(x2d, g, wt, *round_too)


def _cast_weight_tile(w_ref, wbf_ref):
    @pl.when(pl.program_id(1) == 0)
    def _():
        wbf_ref[...] = w_ref[...].astype(wbf_ref.dtype)


def _matmul_kernel(h_ref, w_ref, o_ref, wbf_ref):
    _cast_weight_tile(w_ref, wbf_ref)
    o_ref[...] = _dot_nt(h_ref[...], wbf_ref[...]).astype(o_ref.dtype)


def _weight_rows_spec(tn, d, row_start):
    return pl.BlockSpec((pl.Element(tn), pl.Element(d)), lambda j, i: (pl.multiple_of(row_start(j), 8), 0))


def _matmul_and_round_kernel(h_ref, w_ref, wu_ref, wd_ref, o_ref, wu_bf_ref, wd_bf_ref, wbf_ref):
    wu_bf_ref[...] = wu_ref[...].astype(wu_bf_ref.dtype)
    wd_bf_ref[...] = wd_ref[...].astype(wd_bf_ref.dtype)
    _matmul_kernel(h_ref, w_ref, o_ref, wbf_ref)


def _matmul_t_kernel(h_ref, w_ref, o_ref, wbf_ref):
    _cast_weight_tile(w_ref, wbf_ref)
    res_t = _dot_nt(h_ref[...], wbf_ref[...]).T
    blk = o_ref.shape[-1]
    for kb in range(o_ref.shape[1]):
        o_ref[0, kb] = res_t[:, kb * blk:(kb + 1) * blk].astype(o_ref.dtype)


def _matmul_t(h, wt, row_start, n_blocks, batch, seq, tn, blk, name):
    n, d = h.shape
    return pl.pallas_call(
        _matmul_t_kernel,
        grid=(n_blocks, batch),
        in_specs=[pl.BlockSpec((seq, d), lambda j, i: (i, 0)), _weight_rows_spec(tn, d, row_start)],
        out_specs=pl.BlockSpec((1, seq // blk, tn, blk), lambda j, i: (i, 0, j, 0)),
        out_shape=jax.ShapeDtypeStruct((batch, seq // blk, n_blocks * tn, blk), _BF16),
        scratch_shapes=[pltpu.VMEM((tn, d), _BF16)],
        compiler_params=pltpu.CompilerParams(
            dimension_semantics=("arbitrary", "arbitrary"), vmem_limit_bytes=V7X_VMEM_LIMIT),
        name=name,
    )(h, wt)


def _matmul(h, wt, out_dtype, tm, tn, name, row_start, cols, round_too=None):
    n, d = h.shape
    assert cols % tn == 0
    n_i = n // tm
    steps = (cols // tn) * n_i
    in_specs = [pl.BlockSpec((tm, d), lambda j, i: (i, 0)), _weight_rows_spec(tn, d, row_start)]
    out_specs = [pl.BlockSpec((tm, tn), lambda j, i: (i, j))]
    out_shape = [jax.ShapeDtypeStruct((n, cols), out_dtype)]
    operands = [h, wt]
    kernel_fn = _matmul_kernel
    if round_too is not None:
        w_up, w_down = round_too
        ff = w_up.shape[1] // steps
        assert w_up.shape[1] % steps == 0 and ff % 128 == 0
        slices = [pl.BlockSpec((w_up.shape[0], ff), lambda j, i: (0, j * n_i + i)),
                  pl.BlockSpec((ff, w_down.shape[1]), lambda j, i: (j * n_i + i, 0))]
        in_specs += slices
        out_specs += slices
        out_shape += [jax.ShapeDtypeStruct(w_up.shape, _BF16), jax.ShapeDtypeStruct(w_down.shape, _BF16)]
        operands += [w_up, w_down]
        kernel_fn = _matmul_and_round_kernel
    out = pl.pallas_call(
        kernel_fn,
        grid=(cols // tn, n_i),
        in_specs=in_specs,
        out_specs=out_specs,
        out_shape=out_shape,
        scratch_shapes=[pltpu.VMEM((tn, d), _BF16)],
        compiler_params=pltpu.CompilerParams(
            dimension_semantics=("arbitrary", "arbitrary"), vmem_limit_bytes=V7X_VMEM_LIMIT),
        name=name,
    )(*operands)
    return out[0] if round_too is None else out


def _matmul_classmajor_kernel(h_ref, w_ref, o_ref, acc_ref, tmp_ref, wbf_ref, *, dil, rows):
    _cast_weight_tile(w_ref, wbf_ref)
    res = _dot_nt(h_ref[...], wbf_ref[...])
    if dil == 1:
        o_ref[0, 0] = res.astype(o_ref.dtype)
        return
    for c in range(acc_ref.shape[0]):
        acc_ref[c] = res[:, c * 128:(c + 1) * 128]
    if not _tile_class_major(dil):
        for c in range(acc_ref.shape[0]):
            for r in range(dil):
                o_ref[0, r, :, c * 128:(c + 1) * 128] = (
                    acc_ref[c, pl.ds(r, rows, stride=dil), :].astype(o_ref.dtype))
        return
    s1 = 4
    s2 = dil // s1
    step_rows = rows * s2
    for c in range(acc_ref.shape[0]):
        for r1 in range(s1):
            tmp_ref[c, r1 * step_rows:(r1 + 1) * step_rows, :] = acc_ref[c, pl.ds(r1, step_rows, stride=s1), :]
        for r1 in range(s1):
            for r2 in range(s2):
                o_ref[0, r1 + s1 * r2, :, c * 128:(c + 1) * 128] = (
                    tmp_ref[c, pl.ds(r1 * step_rows + r2, rows, stride=s2), :].astype(o_ref.dtype))


def _matmul_classmajor(h, wt, row_start, n_blocks, dil, batch, seq, tm, tn, name):
    n, d = h.shape
    cols = n_blocks * tn
    tiles_per_seq = seq // tm
    rows = tm // dil
    return pl.pallas_call(
        functools.partial(_matmul_classmajor_kernel, dil=dil, rows=rows),
        grid=(n_blocks, n // tm),
        in_specs=[pl.BlockSpec((tm, d), lambda j, i: (i, 0)), _weight_rows_spec(tn, d, row_start)],
        out_specs=pl.BlockSpec((1, dil, rows, tn),
                               lambda j, i: (i // tiles_per_seq, 0, i % tiles_per_seq, j)),
        out_shape=jax.ShapeDtypeStruct((batch, dil, seq // dil, cols), _BF16),
        scratch_shapes=[pltpu.VMEM((tn // 128, tm, 128), _F32),
                        pltpu.VMEM((tn // 128, tm if _tile_class_major(dil) else 8, 128), _F32),
                        pltpu.VMEM((tn, d), _BF16)],
        compiler_params=pltpu.CompilerParams(
            dimension_semantics=("arbitrary", "arbitrary"), vmem_limit_bytes=V7X_VMEM_LIMIT),
        name=name,
    )(h, wt)


def _tile_class_major(dil):
    return dil % 8 == 0


def _mixer_a_kernel(tab_ref, *refs, seq):
    qkv_refs, o_ref, (bias_ref, o_s, lse_s) = refs[:3 * N_GROUPS], refs[3 * N_GROUPS], refs[3 * N_GROUPS + 1:]
    h = pl.program_id(1)
    scale = HEAD_DIM ** -0.5
    inv_scale = HEAD_DIM ** 0.5
    exp2_coeff = scale * math.log2(math.e)

    @pl.when((pl.program_id(0) == 0) & (h == 0))
    def _build_bias():
        row = lax.broadcasted_iota(jnp.int32, (BAND, BAND), 0)
        col = lax.broadcasted_iota(jnp.int32, (BAND, BAND), 1)
        for side in range(2):
            j = row - col + BAND * (1 - side)
            valid = jnp.where(j >= 0, j, BAND + 1) <= BAND
            for head in range(HEADS_A):
                val = _bias_from_distance(tab_ref, j * DILATIONS[head // HEADS_PER_GROUP], head)
                bias_ref[head, :, side * BAND:(side + 1) * BAND] = jnp.where(valid, val, NEG_INF) * inv_scale

    def key_rows(i):
        return pl.ds(0, BAND) if i == 0 else pl.ds((i - 1) * BAND, 2 * BAND)

    def logits(g, r, i):
        q_ref, k_ref = qkv_refs[3 * g], qkv_refs[3 * g + 1]
        bias = bias_ref[g * HEADS_PER_GROUP + h]
        if i == 0:
            bias = bias[:, BAND:]
        return _dot_nt(q_ref[0, r, pl.ds(i * BAND, BAND), :], k_ref[0, r, key_rows(i), :]) + bias

    def finish(g, r, i, s):
        dil = DILATIONS[g]
        m = jnp.max(s, axis=-1, keepdims=True)
        p = jnp.exp2((s - m) * exp2_coeff).astype(_BF16)
        v = qkv_refs[3 * g + 2][0, r, key_rows(i), :]
        res = jnp.dot(p, jnp.concatenate([v, jnp.ones_like(v)], axis=-1), preferred_element_type=_F32)
        den = res[:, HEAD_DIM:]
        rows = pl.ds(i * BAND, BAND) if dil == 1 else pl.ds(r + i * BAND * dil, BAND, stride=dil)
        o_s[g, rows, :] = res[:, :HEAD_DIM] / den
        lse_s[g, rows, :] = m * scale + jnp.log(den)

    units = [(g, r, i) for g, dil in enumerate(DILATIONS) for r in range(dil) for i in range(seq // dil // BAND)]
    in_flight = 4
    pending = None
    for start in range(0, len(units), in_flight):
        cur = [(g, r, i, logits(g, r, i)) for g, r, i in units[start:start + in_flight]]
        if pending is not None:
            for unit in pending:
                finish(*unit)
        pending = cur
    for unit in pending:
        finish(*unit)

    def mix(c, carry):
        rows = pl.ds(pl.multiple_of(c * MERGE_TILE, MERGE_TILE), MERGE_TILE)
        l0, l1, l2 = lse_s[0, rows, :], lse_s[1, rows, :], lse_s[2, rows, :]
        m = jnp.maximum(jnp.maximum(l0, l1), l2)
        e0, e1, e2 = jnp.exp(l0 - m), jnp.exp(l1 - m), jnp.exp(l2 - m)
        o = (e0 * o_s[0, rows, :] + e1 * o_s[1, rows, :] + e2 * o_s[2, rows, :]) / (e0 + e1 + e2)
        o_ref[0, rows, :] = o.astype(o_ref.dtype)
        return carry

    lax.fori_loop(0, seq // MERGE_TILE, mix, 0)


def _mixer_a(tab, a_groups, batch, seq):
    in_specs = [pl.BlockSpec(memory_space=pltpu.SMEM)]
    operands = [tab]
    for g, dil in enumerate(DILATIONS):
        for which in range(3):
            in_specs.append(pl.BlockSpec((1, dil, seq // dil, HEAD_DIM),
                                         lambda b, h, which=which: (b, 0, 0, which * HEADS_PER_GROUP + h)))
            operands.append(a_groups[g])
    return pl.pallas_call(
        functools.partial(_mixer_a_kernel, seq=seq),
        grid=(batch, HEADS_PER_GROUP),
        in_specs=in_specs,
        out_specs=pl.BlockSpec((1, seq, HEAD_DIM), lambda b, h: (b, 0, h)),
        out_shape=jax.ShapeDtypeStruct((batch, seq, GROUP_W), _BF16),
        scratch_shapes=[pltpu.VMEM((HEADS_A, BAND, 2 * BAND), _F32),
                        pltpu.VMEM((N_GROUPS, seq, HEAD_DIM), _F32),
                        pltpu.VMEM((N_GROUPS, seq, HEAD_DIM), _F32)],
        compiler_params=pltpu.CompilerParams(
            dimension_semantics=("arbitrary", "arbitrary"), vmem_limit_bytes=V7X_VMEM_LIMIT),
        name="mixer_a",
    )(*operands)


def _mixer_b_kernel(tab_ref, lng_ref, lnb_ref, q_ref, kt_ref, v_ref, qidxt_ref, smallk_ref, smallq_ref,
                    o_ref, kidx_s, bias_s, key_s, hi_s, lo_s, selb_s, wt_s, s_s, mx_s, mrep_s, acc_s,
                    *, tq, seq):
    b = pl.program_id(0)
    qi = pl.program_id(1)
    nkb = seq // tq
    lane_tiles = tq // 128
    scale = HEAD_DIM ** -0.5
    inv_scale = HEAD_DIM ** 0.5
    neg_scaled = NEG_INF * inv_scale
    row = lax.broadcasted_iota(jnp.int32, (tq, tq), 0)
    col = lax.broadcasted_iota(jnp.int32, (tq, tq), 1)

    @pl.when((b == 0) & (qi == 0))
    def _build_bias():
        def per_delta(dl, carry):
            dist = row - col + dl * tq
            for h in range(HEADS_B):
                bias_s[dl, h] = _bias_from_distance(tab_ref, dist, HEADS_A + h) * inv_scale
            return carry

        lax.fori_loop(0, nkb, per_delta, 0)

    @pl.when(qi == 0)
    def _layer_norm_keys():
        def chunk(c, carry):
            r = pl.multiple_of(c * tq, tq)
            kx = smallk_ref[0, pl.ds(r, tq), :][:, :IDX_DIM]
            mu = jnp.mean(kx, axis=-1, keepdims=True)
            xc = kx - mu
            var = jnp.mean(xc * xc, axis=-1, keepdims=True)
            y = xc * lax.rsqrt(var + NORM_EPS) * lng_ref[...] + lnb_ref[...]
            z = jnp.zeros_like(y)
            kidx_s[0, pl.ds(r, tq), :] = jnp.concatenate([y, z], axis=-1).astype(_BF16)
            kidx_s[1, pl.ds(r, tq), :] = jnp.concatenate([z, y], axis=-1).astype(_BF16)
            return carry

        lax.fori_loop(0, nkb, chunk, 0)

    wt_s[...] = (smallq_ref[0] * (IDX_HEADS ** -0.5 * IDX_DIM ** -0.5)).T

    def score_blk(kj, carry):
        r = pl.multiple_of(kj * tq, tq)
        acc = jnp.zeros((tq, tq), _F32)
        for h in range(IDX_HEADS):
            pair_t = qidxt_ref[0, 0, (h // 2) * 128:(h // 2 + 1) * 128, :]
            x = jnp.dot(kidx_s[h % 2, pl.ds(r, tq), :], pair_t, preferred_element_type=_F32)
            acc = acc + wt_s[IDX_DIM + h:IDX_DIM + h + 1, :] * jnp.maximum(x, 0.0)
        bits = lax.bitcast_convert_type(acc, jnp.int32)
        key = bits ^ ((bits >> 31) & 0x7FFFFFFF)
        causal = (kj * tq + row) <= (qi * tq + col)
        key = jnp.where(causal, key, INT_MIN)
        key_s[kj] = key
        hi_s[kj] = (key >> 16).astype(jnp.int16)
        lo_s[kj] = ((key & 0xFFFF) - 32768).astype(jnp.int16)
        return carry

    lax.fori_loop(0, qi + 1, score_blk, 0)

    def count_ge(cand):
        def blk(kj, cnt):
            kk = key_s[kj].reshape(tq // 8, 8, tq)
            return cnt + jnp.sum(jnp.where(kk >= cand[None], 1.0, 0.0), axis=0)

        cnt = lax.fori_loop(0, qi + 1, blk, jnp.zeros((8, tq), _F32))
        return jnp.broadcast_to(jnp.sum(cnt, axis=0, keepdims=True), (8, tq))

    one16, zero16, min16 = jnp.ones((), jnp.int16), jnp.zeros((), jnp.int16), jnp.full((), -32768, jnp.int16)

    def packed(cand):
        return jnp.broadcast_to(cand[:1], (16, tq)).astype(jnp.int16)

    def count16(src, cand, strict=False):
        c16 = packed(cand)[None]

        def blk(kj, cnt):
            kk = src[kj].reshape(tq // 16, 16, tq)
            hit = jnp.where((kk > c16) if strict else (kk >= c16), one16, zero16)
            parts = [hit[t] for t in range(tq // 16)]
            while len(parts) > 1:
                parts = [parts[t] + parts[t + 1] for t in range(0, len(parts), 2)]
            return cnt + parts[0]

        cnt = lax.fori_loop(0, qi + 1, blk, jnp.zeros((16, tq), jnp.int16))
        return jnp.broadcast_to(jnp.sum(cnt.astype(_F32), axis=0, keepdims=True), (8, tq))

    def search16(src, target, cnt_floor):
        zero = jnp.zeros((8, tq), jnp.int32)
        cnt0 = count16(src, zero)
        ok0 = cnt0 >= target
        init = (jnp.where(ok0, zero, -32768), jnp.where(ok0, cnt0, cnt_floor))

        def bit_body(bi, carry):
            prefix, cnt_prefix = carry
            cand = prefix | lax.shift_left(jnp.int32(1), 14 - bi)
            cnt = count16(src, cand)
            ok = cnt >= target
            return jnp.where(ok, cand, prefix), jnp.where(ok, cnt, cnt_prefix)

        return lax.fori_loop(0, 15, bit_body, init)

    def kth_largest_key():
        total = jnp.zeros((8, tq), _F32) + ((qi + 1) * tq).astype(_F32)
        tau_hi, cnt_hi = search16(hi_s, float(TOPK), total)
        above_hi = count16(hi_s, tau_hi, strict=True)
        tau_hi16 = packed(tau_hi)[None]

        def mask_lo(kj, carry):
            hi = hi_s[kj].reshape(tq // 16, 16, tq)
            lo = lo_s[kj].reshape(tq // 16, 16, tq)
            lo_s[kj] = jnp.where(hi == tau_hi16, lo, min16).reshape(tq, tq)
            return carry

        lax.fori_loop(0, qi + 1, mask_lo, 0)
        tau_lo, cnt_lo = search16(lo_s, TOPK - above_hi, cnt_hi - above_hi)
        return tau_hi * 65536 + (tau_lo + 32768), above_hi + cnt_lo

    def every_key():
        return jnp.full((8, tq), INT_MIN, jnp.int32), jnp.zeros((8, tq), _F32)

    tau, cnt_tau = lax.cond((qi + 1) * tq > TOPK, kth_largest_key, every_key)
    thr = jnp.maximum(tau, INT_MIN + 1)
    has_ties = jnp.max(jnp.where(tau > INT_MIN, cnt_tau, 0.0)) > TOPK

    @pl.when(jnp.logical_not(has_ties))
    def _select_by_threshold():
        def write_sel(kj, carry):
            kk = key_s[kj].reshape(tq // 8, 8, tq)
            sel_t = jnp.where(kk >= thr[None], 0.0, neg_scaled).reshape(tq, tq)
            selb_s[kj] = sel_t.T
            return carry

        lax.fori_loop(0, qi + 1, write_sel, 0)

    @pl.when(has_ties)
    def _select_breaking_ties():
        need = (TOPK - count_ge(thr + 1))[:1]
        earlier = jnp.where(row > col, 1.0, 0.0).astype(_BF16)

        def write_sel(kj, seen):
            kk = key_s[kj].reshape(tq // 8, 8, tq)
            above = jnp.where(kk > thr[None], 1.0, 0.0).reshape(tq, tq)
            equal = jnp.where(kk == thr[None], 1.0, 0.0).reshape(tq, tq)
            rank = seen + jnp.dot(earlier, equal.astype(_BF16), preferred_element_type=_F32)
            take = above + equal * jnp.where(rank < need, 1.0, 0.0)
            selb_s[kj] = jnp.where(take > 0.0, 0.0, neg_scaled).T
            return seen + jnp.sum(equal, axis=0, keepdims=True)

        lax.fori_loop(0, qi + 1, write_sel, jnp.zeros((1, tq), _F32))

    n_groups, head_group = s_s.shape[:2]
    exp2_coeff = scale * math.log2(math.e)

    def logits_pass(kj, g):
        for hl in range(head_group):
            h = g * head_group + hl
            cs = slice(h * HEAD_DIM, (h + 1) * HEAD_DIM)
            s = (jnp.dot(q_ref[0][:, cs], kt_ref[0, kj, cs, :], preferred_element_type=_F32)
                 + bias_s[qi - kj, h] + selb_s[kj])
            s_s[g, hl, kj] = s
            part = s[:, :128]
            for c in range(1, lane_tiles):
                part = jnp.maximum(part, s[:, c * 128:(c + 1) * 128])
            mx_s[g, hl] = jnp.maximum(mx_s[g, hl], part)

    def value_pass(kj, g):
        r = pl.multiple_of(kj * tq, tq)
        for hl in range(head_group):
            h = g * head_group + hl
            m = mrep_s[g, hl]
            p = jnp.exp2((s_s[g, hl, kj] - jnp.concatenate([m] * lane_tiles, axis=-1)) * exp2_coeff)
            v = v_ref[0, pl.ds(r, tq), h * HEAD_DIM:(h + 1) * HEAD_DIM]
            acc_s[g, hl] = acc_s[g, hl] + jnp.dot(
                p.astype(_BF16), jnp.concatenate([v, jnp.ones_like(v)], axis=-1), preferred_element_type=_F32)

    def start_group(g):
        for hl in range(head_group):
            mx_s[g, hl] = jnp.full((tq, 128), -3e38, _F32)

    def close_logits(g):
        for hl in range(head_group):
            m = jnp.max(mx_s[g, hl], axis=-1, keepdims=True)
            mrep_s[g, hl] = jnp.broadcast_to(m, (tq, 128))
            acc_s[g, hl] = jnp.zeros((tq, 2 * HEAD_DIM), _F32)

    def close_values(g):
        for hl in range(head_group):
            h = g * head_group + hl
            acc = acc_s[g, hl]
            o_ref[0, :, h * HEAD_DIM:(h + 1) * HEAD_DIM] = (acc[:, :HEAD_DIM] / acc[:, HEAD_DIM:]).astype(o_ref.dtype)

    for g in range(n_groups + 1):
        if g < n_groups:
            start_group(g)

        def blk(kj, carry, g=g):
            if g < n_groups:
                logits_pass(kj, g)
            if g > 0:
                value_pass(kj, g - 1)
            return carry

        lax.fori_loop(0, qi + 1, blk, 0)
        if g < n_groups:
            close_logits(g)
        if g > 0:
            close_values(g - 1)


def _mixer_b(tab, ln_g, ln_b, qv3, kq_t, small3, tq):
    batch, seq, _ = qv3.shape
    nkb = seq // tq
    head_group = 4
    small_cb = small3.shape[-1] // SMALL_W - 1
    return pl.pallas_call(
        functools.partial(_mixer_b_kernel, tq=tq, seq=seq),
        grid=(batch, nkb),
        in_specs=[
            pl.BlockSpec(memory_space=pltpu.SMEM),
            pl.BlockSpec((1, IDX_DIM), lambda b, i: (0, 0)),
            pl.BlockSpec((1, IDX_DIM), lambda b, i: (0, 0)),
            pl.BlockSpec((1, tq, WIDTH_B), lambda b, i: (b, i, 0)),
            pl.BlockSpec((1, nkb, WIDTH_B, tq), lambda b, i: (b, 0, 0, 0), pipeline_mode=pl.Buffered(1)),
            pl.BlockSpec((1, seq, WIDTH_B), lambda b, i: (b, 0, 1), pipeline_mode=pl.Buffered(1)),
            pl.BlockSpec((1, 1, IDX_W, tq), lambda b, i: (b, i, 1, 0)),
            pl.BlockSpec((1, seq, SMALL_W), lambda b, i: (b, 0, small_cb), pipeline_mode=pl.Buffered(1)),
            pl.BlockSpec((1, tq, SMALL_W), lambda b, i: (b, i, small_cb)),
        ],
        out_specs=pl.BlockSpec((1, tq, WIDTH_B), lambda b, i: (b, i, 0)),
        out_shape=jax.ShapeDtypeStruct((batch, seq, WIDTH_B), _BF16),
        scratch_shapes=[
            pltpu.VMEM((2, seq, 2 * IDX_DIM), _BF16),
            pltpu.VMEM((nkb, HEADS_B, tq, tq), _F32),
            pltpu.VMEM((nkb, tq, tq), jnp.int32),
            pltpu.VMEM((nkb, tq, tq), jnp.int16),
            pltpu.VMEM((nkb, tq, tq), jnp.int16),
            pltpu.VMEM((nkb, tq, tq), _F32),
            pltpu.VMEM((SMALL_W, tq), _F32),
            pltpu.VMEM((HEADS_B // head_group, head_group, nkb, tq, tq), _F32),
            pltpu.VMEM((HEADS_B // head_group, head_group, tq, 128), _F32),
            pltpu.VMEM((HEADS_B // head_group, head_group, tq, 128), _F32),
            pltpu.VMEM((HEADS_B // head_group, head_group, tq, 2 * HEAD_DIM), _F32),
        ],
        compiler_params=pltpu.CompilerParams(
            dimension_semantics=("arbitrary", "arbitrary"), vmem_limit_bytes=V7X_VMEM_LIMIT),
        name="mixer_b",
    )(tab, ln_g, ln_b, qv3, kq_t, qv3, kq_t, small3, small3)


def _merge_kernel(x_ref, oa_ref, ob_ref, ga_ref, gb_ref, wpa_ref, wpb_ref, wout_ref, y_ref):
    pa = jnp.dot(oa_ref[...], wpa_ref[...], preferred_element_type=_F32)
    pb = jnp.dot(ob_ref[...], wpb_ref[...], preferred_element_type=_F32)
    merged = jax.nn.sigmoid(ga_ref[...]) * pa + jax.nn.sigmoid(gb_ref[...]) * pb
    y_ref[...] = x_ref[...] + jnp.dot(merged.astype(_BF16), wout_ref[...], preferred_element_type=_F32)


def _merge(x2d, oa2d, ob2d, gates, wpa, wpb, wout, tm):
    n, d = x2d.shape
    rows = lambda w, cb=0: pl.BlockSpec((tm, w), lambda i, cb=cb: (i, cb))
    const = lambda shape: pl.BlockSpec(shape, lambda i: (0, 0), pipeline_mode=pl.Buffered(1))
    return pl.pallas_call(
        _merge_kernel,
        grid=(n // tm,),
        in_specs=[rows(d), rows(GROUP_W), rows(WIDTH_B), rows(d, 0), rows(d, 1),
                  const(wpa.shape), const(wpb.shape), const(wout.shape)],
        out_specs=rows(d),
        out_shape=jax.ShapeDtypeStruct((n, d), _F32),
        compiler_params=pltpu.CompilerParams(
            dimension_semantics=("arbitrary",), vmem_limit_bytes=V7X_VMEM_LIMIT),
        name="merge_out_proj",
    )(x2d, oa2d, ob2d, gates, gates, wpa, wpb, wout)


def _mlp_kernel(x_ref, g_ref, gf_ref, wu_ref, wd_ref, y_ref, h_ref):
    j = pl.program_id(1)

    @pl.when(j == 0)
    def _():
        x = x_ref[...]
        ms = jnp.mean(x * x, axis=-1, keepdims=True)
        h_ref[...] = (x * lax.rsqrt(ms + NORM_EPS) * g_ref[...]).astype(_BF16)
        y_ref[...] = x

    u = jnp.maximum(jnp.dot(h_ref[...], wu_ref[...], preferred_element_type=_F32), 0.0)
    y_ref[...] += jnp.dot((u * u).astype(_BF16), wd_ref[...], preferred_element_type=_F32)

    @pl.when(j == pl.num_programs(1) - 1)
    def _():
        y = y_ref[...]
        ms = jnp.mean(y * y, axis=-1, keepdims=True)
        y_ref[...] = y * lax.rsqrt(ms + NORM_EPS) * gf_ref[...]


def _mlp(x2d, g, gf, wu, wd, tm, tf):
    n, d = x2d.shape
    dff = wu.shape[1]
    return pl.pallas_call(
        _mlp_kernel,
        grid=(n // tm, dff // tf),
        in_specs=[
            pl.BlockSpec((tm, d), lambda i, j: (i, 0)),
            pl.BlockSpec((1, d), lambda i, j: (0, 0)),
            pl.BlockSpec((1, d), lambda i, j: (0, 0)),
            pl.BlockSpec((d, tf), lambda i, j: (0, j)),
            pl.BlockSpec((tf, d), lambda i, j: (j, 0)),
        ],
        out_specs=pl.BlockSpec((tm, d), lambda i, j: (i, 0)),
        out_shape=jax.ShapeDtypeStruct((n, d), _F32),
        scratch_shapes=[pltpu.VMEM((tm, d), _BF16)],
        compiler_params=pltpu.CompilerParams(
            dimension_semantics=("arbitrary", "arbitrary"), vmem_limit_bytes=V7X_VMEM_LIMIT),
        name="mlp_final_norm",
    )(x2d, g, gf, wu, wd)


def _largest_tile(n, limit, step):
    t = (min(n, limit) // step) * step
    while n % t:
        t -= step
    return t


def kernel(x, norm_mix_g, w_in, idx_k_norm_g, idx_k_norm_b, rel_bias_table, w_proj_a, w_proj_b, w_out,
           norm_mlp_g, w_mlp_up, w_mlp_down, norm_final_g):
    batch, seq, d = x.shape
    assert seq == MAX_DISTANCE and seq % (DILATIONS[-1] * BAND) == 0
    assert w_in.shape[0] == 1, "single layer"
    n = batch * seq
    x2d = x.reshape(n, d)
    tab = rel_bias_table.reshape(-1)

    b0 = 3 * WIDTH_A
    k0 = b0 + 3 * WIDTH_B + IDX_W
    ga0 = k0 + IDX_DIM + IDX_HEADS
    wt = w_in[0].T

    h, small, wpa, wpb, wout = _rmsnorm(x2d, norm_mix_g[0].reshape(1, d), wt, k0, _largest_tile(n, 1024, 8),
                                        round_too=(w_proj_a[0], w_proj_b[0], w_out[0]))
    per = WIDTH_B // GROUP_W
    qv_b, w_up, w_down = _matmul(
        h, wt, _BF16, seq, GROUP_W, "proj_qv_b", cols=2 * WIDTH_B,
        row_start=lambda j: b0 + (j // per) * 2 * WIDTH_B + (j % per) * GROUP_W,
        round_too=(w_mlp_up[0], w_mlp_down[0]))
    tq = 256
    kq_t = _matmul_t(h, wt, lambda j: b0 + WIDTH_B + (j // per) * 2 * WIDTH_B + (j % per) * GROUP_W,
                     2 * per, batch, seq, GROUP_W, tq, "proj_kq_t")
    gates_tn = _largest_tile(2 * d, 1024, 128)
    gates = _matmul(h, wt, _F32, _largest_tile(seq, 1024, 256), gates_tn, "proj_gates",
                    row_start=lambda j: ga0 + j * gates_tn, cols=2 * d)

    a_groups = [_matmul_classmajor(h, wt, lambda j, group=group: (j * N_GROUPS + group) * GROUP_W, 3,
                                   DILATIONS[group], batch, seq, seq, GROUP_W, f"proj_a_g{group}")
                for group in range(N_GROUPS)]
    oa = _mixer_a(tab, a_groups, batch, seq)

    ob = _mixer_b(tab, idx_k_norm_g[0].reshape(1, IDX_DIM), idx_k_norm_b[0].reshape(1, IDX_DIM),
                  qv_b.reshape(batch, seq, -1), kq_t, small.reshape(batch, seq, -1), tq=tq)

    x1 = _merge(x2d, oa.reshape(n, GROUP_W), ob.reshape(n, WIDTH_B), gates, wpa, wpb, wout, tm=MERGE_TILE)
    y = _mlp(x1, norm_mlp_g[0].reshape(1, d), norm_final_g.reshape(1, d), w_up, w_down,
             tm=_largest_tile(n, 1024, 128), tf=_largest_tile(w_mlp_up.shape[-1], 512, 128))
    return y.reshape(batch, seq, d)
```
